```python
import jax, jax.numpy as jnp
from jax import lax
import numpy as np

D_MODEL = 1024
BATCH = 16
SEQ = 2048
DEPTH = 4

HEAD_DIM = 64
ROPE_THETA = 10000.0
A_HEADS = 8
A_CONFIGS = ((128, 1), (512, 4), (2048, 16))
B_HEADS = 8
B_KV_HEADS = 2
B_WINDOW = 128
C_HEADS = 8
IDX_HEADS = 8
IDX_DIM = 64
TOPK_MAX = 256
QUERY_BLOCK = 128
BAND_BLOCK = 128
D_FF = 2816
LN_EPS = 1e-5
DN_ALPHA = (2 * DEPTH) ** 0.25
DN_BETA = (8 * DEPTH) ** -0.25

A_W = A_HEADS * HEAD_DIM
B_QW = B_HEADS * HEAD_DIM
B_KW = B_KV_HEADS * HEAD_DIM
C_W = C_HEADS * HEAD_DIM
IN_SIZES = (A_W, A_W, A_W,
            B_QW, B_KW, B_KW,
            C_W, HEAD_DIM, HEAD_DIM,
            IDX_HEADS * IDX_DIM, IDX_DIM, IDX_HEADS,
            D_MODEL, D_MODEL, D_MODEL)
D_IN = sum(IN_SIZES)

kernel_name = "hybrid_dilated_swa_dsa_macaron_deepnorm"


def layer_norm(x, g, b):
    xf = x.astype(jnp.float32)
    mu = xf.mean(-1, keepdims=True)
    var = jnp.square(xf - mu).mean(-1, keepdims=True)
    y = (xf - mu) * lax.rsqrt(var + LN_EPS)
    return (y * g.astype(jnp.float32) + b.astype(jnp.float32)).astype(x.dtype)


def rope_tables(positions, dim):
    inv = ROPE_THETA ** (-jnp.arange(0, dim, 2, dtype=jnp.float32) / dim)
    ang = positions.astype(jnp.float32)[..., None] * inv
    return jnp.cos(ang), jnp.sin(ang)


def apply_rope(t, cos, sin):
    tf = t.astype(jnp.float32)
    t1, t2 = jnp.split(tf, 2, axis=-1)
    c = cos[:, :, None, :]
    s = sin[:, :, None, :]
    return jnp.concatenate([t1 * c - t2 * s, t2 * c + t1 * s], axis=-1).astype(t.dtype)


def swiglu(x, w_in, w_out):
    gate, up = jnp.split(x @ w_in, 2, axis=-1)
    return (jax.nn.silu(gate) * up) @ w_out


def banded_attention(q, k, v, max_dist, sink=None):
    n, l, h, dh = q.shape
    g = k.shape[2]
    r = h // g
    blk = BAND_BLOCK
    nb = -(-l // blk)
    pad = nb * blk - l
    qp = jnp.pad(q, ((0, 0), (0, pad), (0, 0), (0, 0))).reshape(n, nb, blk, g, r, dh)

    def kv_blocks(t):
        tp = jnp.pad(t, ((0, 0), (blk, pad), (0, 0), (0, 0))).reshape(n, nb + 1, blk, g, dh)
        return jnp.concatenate([tp[:, :-1], tp[:, 1:]], axis=2)

    kb, vb = kv_blocks(k), kv_blocks(v)
    s = jnp.einsum('nbqgrd,nbkgd->nbgrqk', qp, kb, preferred_element_type=jnp.float32) * (dh ** -0.5)
    qi = jnp.arange(blk)[:, None]
    kj = jnp.arange(2 * blk)[None, :]
    dist = qi + blk - kj
    kpos = jnp.arange(nb)[:, None, None] * blk - blk + kj[None]
    valid = (dist >= 0) & (dist <= max_dist) & (kpos >= 0)
    s = jnp.where(valid[:, None, None], s, -jnp.inf)
    m = s.max(-1)
    if sink is not None:
        sk = sink.astype(jnp.float32).reshape(g, r)[:, :, None]
        m = jnp.maximum(m, sk)
    p = jnp.exp(s - m[..., None])
    den = p.sum(-1)
    if sink is not None:
        den = den + jnp.exp(sk - m)
    o = jnp.einsum('nbgrqk,nbkgd->nbqgrd', p.astype(v.dtype), vb, preferred_element_type=jnp.float32)
    o = o / den.transpose(0, 1, 4, 2, 3)[..., None]
    o = o.reshape(n, nb * blk, h, dh)[:, :l].astype(q.dtype)
    lse = (m + jnp.log(den)).transpose(0, 1, 4, 2, 3).reshape(n, nb * blk, h)[:, :l]
    return o, lse


def dilated_attention(q, k, v):
    b, s, h, dh = q.shape
    outs, lses = [], []
    for window, dil in A_CONFIGS:
        def fold(t):
            return t.reshape(b, s // dil, dil, h, dh).transpose(0, 2, 1, 3, 4).reshape(b * dil, s // dil, h, dh)
        o, lse = banded_attention(fold(q), fold(k), fold(v), window // dil)
        outs.append(o.reshape(b, dil, s // dil, h, dh).transpose(0, 2, 1, 3, 4).reshape(b, s, h, dh))
        lses.append(lse.reshape(b, dil, s // dil, h).transpose(0, 2, 1, 3).reshape(b, s, h))
    wts = jax.nn.softmax(jnp.stack(lses, 0), axis=0)
    o = jnp.einsum('cbsh,cbshd->bshd', wts, jnp.stack(outs, 0).astype(jnp.float32))
    return o.astype(q.dtype)


def dsa_attention(q, k, v, q_idx, k_idx, w_idx):
    b, s, h, dh = q.shape
    k_sel = min(TOPK_MAX, s // 4)
    nq = s // QUERY_BLOCK
    key_pos = jnp.arange(s)
    gather = jax.vmap(lambda t, ix: t[ix])

    def block(i):
        start = i * QUERY_BLOCK
        qb = lax.dynamic_slice_in_dim(q, start, QUERY_BLOCK, axis=1)
        qib = lax.dynamic_slice_in_dim(q_idx, start, QUERY_BLOCK, axis=1)
        wb = lax.dynamic_slice_in_dim(w_idx, start, QUERY_BLOCK, axis=1)
        qpos = start + jnp.arange(QUERY_BLOCK)
        causal = key_pos[None, :] <= qpos[:, None]
        dots = jnp.einsum('bqhd,bsd->bqhs', qib, k_idx, preferred_element_type=jnp.float32)
        score = jnp.einsum('bqhs,bqh->bqs', jax.nn.relu(dots), wb.astype(jnp.float32))
        score = jnp.where(causal[None], score, -jnp.inf)
        _, idx = lax.top_k(score, k_sel)
        sel_valid = idx <= qpos[None, :, None]
        kg = gather(k, idx)
        vg = gather(v, idx)
        att = jnp.einsum('bqhd,bqkd->bhqk', qb, kg, preferred_element_type=jnp.float32) * (dh ** -0.5)
        att = jnp.where(sel_valid[:, None], att, -jnp.inf)
        p = jax.nn.softmax(att, axis=-1)
        return jnp.einsum('bhqk,bqkd->bqhd', p.astype(v.dtype), vg,
                          preferred_element_type=jnp.float32).astype(q.dtype)

    out = lax.map(block, jnp.arange(nq))
    return out.transpose(1, 0, 2, 3, 4).reshape(b, s, h, dh)


def hybrid_mixer(x, cos, sin, w_in, sink_b, w_br_a, w_br_b, w_br_c, w_out):
    b, s, _ = x.shape
    proj = x @ w_in
    splits = [int(c) for c in np.cumsum(IN_SIZES)[:-1]]
    (qa, ka, va, qb, kb, vb, qc, kc, vc, qi, ki, wi, ga, gb, gc) = jnp.split(proj, splits, axis=-1)

    def heads(t, nh):
        return t.reshape(b, s, nh, -1)

    o_a = dilated_attention(apply_rope(heads(qa, A_HEADS), cos, sin),
                            apply_rope(heads(ka, A_HEADS), cos, sin),
                            heads(va, A_HEADS)).reshape(b, s, A_W)
    o_b, _ = banded_attention(apply_rope(heads(qb, B_HEADS), cos, sin),
                              apply_rope(heads(kb, B_KV_HEADS), cos, sin),
                              heads(vb, B_KV_HEADS), B_WINDOW - 1, sink_b)
    o_b = o_b.reshape(b, s, B_QW)
    o_c = dsa_attention(apply_rope(heads(qc, C_HEADS), cos, sin),
                        apply_rope(heads(kc, 1), cos, sin)[:, :, 0],
                        vc,
                        apply_rope(heads(qi, IDX_HEADS), cos, sin),
                        apply_rope(heads(ki, 1), cos, sin)[:, :, 0],
                        wi).reshape(b, s, C_W)
    merged = (jax.nn.sigmoid(ga) * (o_a @ w_br_a)
              + jax.nn.sigmoid(gb) * (o_b @ w_br_b)
              + jax.nn.sigmoid(gc) * (o_c @ w_br_c))
    return merged @ w_out


def setup_inputs(seed: int = 0) -> dict:
    key = jax.random.key(seed)
    ks = jax.random.split(key, 16)
    f32 = jnp.float32

    def nrm(k, shape, scale):
        return jax.random.normal(k, shape, f32) * scale

    x = jax.random.normal(ks[0], (BATCH, SEQ, D_MODEL), f32)
    offs = jax.random.randint(ks[1], (BATCH, 1), 0, 4096, dtype=jnp.int32)
    positions = jnp.arange(SEQ, dtype=jnp.int32)[None, :] + offs
    return {
        "x": x,
        "positions": positions,
        "w_in": nrm(ks[2], (DEPTH, D_MODEL, D_IN), D_MODEL ** -0.5),
        "sink_b": nrm(ks[3], (DEPTH, B_HEADS), 0.5),
        "w_br_a": nrm(ks[4], (DEPTH, A_W, D_MODEL), A_W ** -0.5),
        "w_br_b": nrm(ks[5], (DEPTH, B_QW, D_MODEL), B_QW ** -0.5),
        "w_br_c": nrm(ks[6], (DEPTH, C_W, D_MODEL), C_W ** -0.5),
        "w_out": nrm(ks[7], (DEPTH, D_MODEL, D_MODEL), D_MODEL ** -0.5 * DN_BETA),
        "ffn1_in": nrm(ks[8], (DEPTH, D_MODEL, 2 * D_FF), D_MODEL ** -0.5),
        "ffn1_out": nrm(ks[9], (DEPTH, D_FF, D_MODEL), D_FF ** -0.5 * DN_BETA),
        "ffn2_in": nrm(ks[10], (DEPTH, D_MODEL, 2 * D_FF), D_MODEL ** -0.5),
        "ffn2_out": nrm(ks[11], (DEPTH, D_FF, D_MODEL), D_FF ** -0.5 * DN_BETA),
        "ln_g": 1.0 + nrm(ks[12], (DEPTH, 3, D_MODEL), 0.02),
        "ln_b": nrm(ks[13], (DEPTH, 3, D_MODEL), 0.02),
    }


def reference(x, positions, w_in, sink_b, w_br_a, w_br_b, w_br_c, w_out,
              ffn1_in, ffn1_out, ffn2_in, ffn2_out, ln_g, ln_b):
    cos, sin = rope_tables(positions, HEAD_DIM)
    for l in range(DEPTH):
        x = layer_norm(DN_ALPHA * x + 0.5 * swiglu(x, ffn1_in[l], ffn1_out[l]), ln_g[l, 0], ln_b[l, 0])
        x = layer_norm(DN_ALPHA * x + hybrid_mixer(x, cos, sin, w_in[l], sink_b[l], w_br_a[l],
                                                   w_br_b[l], w_br_c[l], w_out[l]),
                       ln_g[l, 1], ln_b[l, 1])
        x = layer_norm(DN_ALPHA * x + 0.5 * swiglu(x, ffn2_in[l], ffn2_out[l]), ln_g[l, 2], ln_b[l, 2])
    return x
```

```python
import functools

import jax
import jax.numpy as jnp
import numpy as np
from jax import lax
from jax.experimental import pallas as pl
from jax.experimental.pallas import tpu as pltpu

HEAD_DIM = 64
ROPE_THETA = 10000.0
A_HEADS = 8
A_CONFIGS = ((128, 1), (512, 4), (2048, 16))
B_HEADS = 8
B_KV_HEADS = 2
B_WINDOW = 128
C_HEADS = 8
IDX_HEADS = 8
TOPK_MAX = 256
BLOCK = 128
LN_EPS = 1e-5
LANES = 128
VMEM_LIMIT_BYTES = 56 * 1024 * 1024
INT_MIN = -(2 ** 31)

F32 = jnp.float32
BF16 = jnp.bfloat16


def _params(*sem):
    return pltpu.CompilerParams(dimension_semantics=sem, vmem_limit_bytes=VMEM_LIMIT_BYTES)


def _layer_norm(y, g, b):
    mu = jnp.mean(y, axis=-1, keepdims=True)
    d = y - mu
    var = jnp.mean(d * d, axis=-1, keepdims=True)
    return d * lax.rsqrt(var + LN_EPS) * g + b


def _rope_table_kernel(pos_ref, inv_ref, sign_ref, cos_ref, sin_ref):
    ang = pos_ref[...].astype(F32) * inv_ref[...]
    cos_ref[...] = jnp.cos(ang)
    sin_ref[...] = jnp.sin(ang) * sign_ref[...]


def _rope_tables(positions):
    n = positions.size
    tm = 1024
    inv = ROPE_THETA ** (-jnp.arange(0, HEAD_DIM, 2, dtype=F32) / HEAD_DIM)
    inv = jnp.tile(inv, LANES // inv.shape[0])[None, :]
    half = (np.arange(LANES) % HEAD_DIM) < HEAD_DIM // 2
    sign = jnp.asarray(np.where(half, -1.0, 1.0), F32)[None, :]
    row = pl.BlockSpec((tm, LANES), lambda i: (i, 0))
    const = pl.BlockSpec((1, LANES), lambda i: (0, 0))
    return pl.pallas_call(
        _rope_table_kernel,
        grid=(n // tm,),
        in_specs=[pl.BlockSpec((tm, 1), lambda i: (i, 0)), const, const],
        out_specs=[row, row],
        out_shape=[jax.ShapeDtypeStruct((n, LANES), F32)] * 2,
        compiler_params=_params("parallel"),
        name="rope_tables",
    )(positions.reshape(n, 1), inv, sign)


def _ffn_kernel(x_ref, wg_ref, wu_ref, wo_ref, g_ref, b_ref, o_ref, acc_ref, *, alpha, nf):
    j = pl.program_id(1)
    xb = x_ref[...].astype(BF16)
    gate = jnp.dot(xb, wg_ref[...], preferred_element_type=F32)
    up = jnp.dot(xb, wu_ref[...], preferred_element_type=F32)
    h = (gate * jax.nn.sigmoid(gate) * up).astype(BF16)
    part = jnp.dot(h, wo_ref[...], preferred_element_type=F32)

    @pl.when(j == 0)
    def _():
        acc_ref[...] = part

    @pl.when(j > 0)
    def _():
        acc_ref[...] += part

    @pl.when(j == nf - 1)
    def _():
        y = alpha * x_ref[...] + 0.5 * acc_ref[...]
        o_ref[...] = _layer_norm(y, g_ref[...], b_ref[...])


def _ffn(x, w_in, w_out, g, b, alpha):
    n, d = x.shape
    d_ff = w_out.shape[0]
    tm = 512
    nf = 2
    tf = d_ff // nf
    return pl.pallas_call(
        functools.partial(_ffn_kernel, alpha=alpha, nf=nf),
        grid=(n // tm, nf),
        in_specs=[
            pl.BlockSpec((tm, d), lambda i, j: (i, 0)),
            pl.BlockSpec((d, tf), lambda i, j: (0, j)),
            pl.BlockSpec((d, tf), lambda i, j: (0, nf + j)),
            pl.BlockSpec((tf, d), lambda i, j: (j, 0)),
            pl.BlockSpec((1, d), lambda i, j: (0, 0)),
            pl.BlockSpec((1, d), lambda i, j: (0, 0)),
        ],
        out_specs=pl.BlockSpec((tm, d), lambda i, j: (i, 0)),
        out_shape=jax.ShapeDtypeStruct((n, d), F32),
        scratch_shapes=[pltpu.VMEM((tm, d), F32)],
        compiler_params=_params("parallel", "arbitrary"),
        name="ffn_ln",
    )(x, w_in, w_in, w_out, g, b)


A_W = A_HEADS * HEAD_DIM
PACK_A = 3 * A_W
PACK_B = B_HEADS * HEAD_DIM + 4 * LANES
PACK_C = 2 * C_HEADS * HEAD_DIM + 3 * LANES
C_KC, C_KI, C_VC = 1024, 1152, 1280


def _pack_in_weights(w):
    sizes = (A_W, A_W, A_W, 512, 128, 128, 512, 64, 64, 512, 64, 8, 1024, 1024, 1024)
    offs = np.concatenate([[0], np.cumsum(sizes)])
    (qa, ka, va, qb, kb, vb, qc, kc, vc, qi, ki, wi, ga, gb, gc) = [
        w[:, offs[i]:offs[i + 1]] for i in range(len(sizes))]

    def dup_heads(t):
        g = t.shape[1] // HEAD_DIM
        t = t.reshape(t.shape[0], g, 1, HEAD_DIM)
        return jnp.broadcast_to(t, (t.shape[0], g, 2, HEAD_DIM)).reshape(t.shape[0], g * LANES)

    wi_pad = jnp.pad(wi, ((0, 0), (0, LANES - wi.shape[1])))
    packed = jnp.concatenate(
        [qa, ka, va, qb, dup_heads(kb), dup_heads(vb),
         qc, qi, dup_heads(kc), dup_heads(ki), dup_heads(vc), wi_pad], axis=1)
    gates = jnp.concatenate([ga, gb, gc], axis=1)
    return packed.astype(BF16), gates.astype(BF16)


def _rope(h, cos, sin):
    w = h.shape[1]
    reps = w // LANES
    c = jnp.tile(cos, (1, reps)) if reps > 1 else cos
    s = jnp.tile(sin, (1, reps)) if reps > 1 else sin
    lane = lax.broadcasted_iota(jnp.int32, h.shape, 1)
    first = (lane % HEAD_DIM) < HEAD_DIM // 2
    partner = jnp.where(first, pltpu.roll(h, w - HEAD_DIM // 2, 1), pltpu.roll(h, HEAD_DIM // 2, 1))
    return h * c + partner * s


def _inproj_kernel(x_ref, w_ref, cos_ref, sin_ref, a_ref, b_ref, c_ref, wi_ref):
    xb = x_ref[...].astype(BF16)
    cos = cos_ref[...]
    sin = sin_ref[...]
    scale = HEAD_DIM ** -0.5

    def proj(c0, width):
        return jnp.dot(xb, w_ref[:, c0:c0 + width], preferred_element_type=F32)

    a_ref[:, 0:512] = (_rope(proj(0, 512), cos, sin) * scale).astype(BF16)
    a_ref[:, 512:1024] = _rope(proj(512, 512), cos, sin).astype(BF16)
    a_ref[:, 1024:1536] = proj(1024, 512).astype(BF16)
    o = PACK_A
    b_ref[:, 0:512] = (_rope(proj(o, 512), cos, sin) * scale).astype(BF16)
    b_ref[:, 512:768] = _rope(proj(o + 512, 256), cos, sin).astype(BF16)
    b_ref[:, 768:1024] = proj(o + 768, 256).astype(BF16)
    o = PACK_A + PACK_B
    c_ref[:, 0:512] = (_rope(proj(o, 512), cos, sin) * scale).astype(BF16)
    c_ref[:, 512:1024] = _rope(proj(o + 512, 512), cos, sin).astype(BF16)
    c_ref[:, 1024:1280] = _rope(proj(o + 1024, 256), cos, sin).astype(BF16)
    c_ref[:, 1280:1408] = proj(o + 1280, 128).astype(BF16)
    wi_ref[...] = proj(o + 1408, 128)


def _inproj(x, w_packed, cos, sin):
    n, d = x.shape
    tm = 512
    wtot = w_packed.shape[1]
    row = lambda w: pl.BlockSpec((tm, w), lambda i: (i, 0))
    return pl.pallas_call(
        _inproj_kernel,
        grid=(n // tm,),
        in_specs=[row(d), pl.BlockSpec((d, wtot), lambda i: (0, 0)), row(LANES), row(LANES)],
        out_specs=[row(PACK_A), row(PACK_B), row(PACK_C), row(LANES)],
        out_shape=[
            jax.ShapeDtypeStruct((n, PACK_A), BF16),
            jax.ShapeDtypeStruct((n, PACK_B), BF16),
            jax.ShapeDtypeStruct((n, PACK_C), BF16),
            jax.ShapeDtypeStruct((n, LANES), F32),
        ],
        compiler_params=_params("parallel"),
        name="inproj_rope",
    )(x, w_packed, cos, sin)


def _banded_kernel(*refs, n_tiles, kv_tile_of, max_dist, has_sink, with_lse):
    q_ref, kp_ref, kc_ref, vp_ref, vc_ref = refs[:5]
    rest = refs[5:]
    if has_sink:
        sink_ref, rest = rest[0], rest[1:]
    o_ref = rest[0]
    lse_ref = rest[1] if with_lse else None

    j = pl.program_id(1)
    qi = lax.broadcasted_iota(jnp.int32, (BLOCK, 2 * BLOCK), 0)
    kj = lax.broadcasted_iota(jnp.int32, (BLOCK, 2 * BLOCK), 1)
    dist = qi + BLOCK - kj
    valid = (dist >= 0) & (dist <= max_dist) & ((kj >= BLOCK) | (j > 0))
    lane = lax.broadcasted_iota(jnp.int32, (1, LANES), 1)
    lo = lane < HEAD_DIM

    for t in range(n_tiles):
        g = kv_tile_of(t)
        q = q_ref[0, :, t * LANES:(t + 1) * LANES].astype(F32)
        k2 = jnp.concatenate([kp_ref[0, :, g * LANES:(g + 1) * LANES],
                              kc_ref[0, :, g * LANES:(g + 1) * LANES]], axis=0)
        v2 = jnp.concatenate([vp_ref[0, :, g * LANES:(g + 1) * LANES],
                              vc_ref[0, :, g * LANES:(g + 1) * LANES]], axis=0).astype(F32)
        acc = None
        dens = []
        lses = []
        for par in range(2):
            keep = lo if par == 0 else jnp.logical_not(lo)
            qm = jnp.where(keep, q, 0.0).astype(BF16)
            s = lax.dot_general(qm, k2, (((1,), (1,)), ((), ())), preferred_element_type=F32)
            s = jnp.where(valid, s, -jnp.inf)
            m = jnp.max(s, axis=-1, keepdims=True)
            if has_sink:
                h = 2 * t + par
                sk = sink_ref[0:1, h:h + 1]
                m = jnp.maximum(m, sk)
            p = jnp.exp(s - m)
            den = jnp.sum(p, axis=-1, keepdims=True)
            if has_sink:
                den = den + jnp.exp(sk - m)
            vm = jnp.where(keep, v2, 0.0).astype(BF16)
            o = jnp.dot(p.astype(BF16), vm, preferred_element_type=F32)
            acc = o if acc is None else acc + o
            dens.append(den)
            lses.append(m + jnp.log(den))
        den2 = jnp.where(lo, dens[0], dens[1])
        o_ref[0, :, t * LANES:(t + 1) * LANES] = (acc / den2).astype(o_ref.dtype)
        if with_lse:
            lse_ref[0, :, t * LANES:(t + 1) * LANES] = jnp.where(lo, lses[0], lses[1])


def _banded_attention(qkv, *, q_col, k_col, v_col, q_width, kv_width, kv_tile_of, max_dist,
                      sink=None, with_lse=False, out_dtype=BF16):
    n, l, _ = qkv.shape
    nb = l // BLOCK
    n_tiles = q_width // LANES
    qs = pl.BlockSpec((1, BLOCK, q_width), lambda b, j: (b, j, q_col // q_width))
    cur = lambda col: pl.BlockSpec((1, BLOCK, kv_width), lambda b, j: (b, j, col // kv_width))
    prev = lambda col: pl.BlockSpec((1, BLOCK, kv_width),
                                    lambda b, j: (b, jnp.maximum(j - 1, 0), col // kv_width))
    in_specs = [qs, prev(k_col), cur(k_col), prev(v_col), cur(v_col)]
    args = [qkv] * 5
    if sink is not None:
        in_specs.append(pl.BlockSpec((1, sink.shape[-1]), lambda b, j: (0, 0)))
        args.append(sink.reshape(1, -1).astype(F32))
    os = pl.BlockSpec((1, BLOCK, q_width), lambda b, j: (b, j, 0))
    out_specs = [os]
    out_shape = [jax.ShapeDtypeStruct((n, l, q_width), out_dtype)]
    if with_lse:
        out_specs.append(os)
        out_shape.append(jax.ShapeDtypeStruct((n, l, q_width), F32))
    res = pl.pallas_call(
        functools.partial(_banded_kernel, n_tiles=n_tiles, kv_tile_of=kv_tile_of,
                          max_dist=max_dist, has_sink=sink is not None, with_lse=with_lse),
        grid=(n, nb),
        in_specs=in_specs,
        out_specs=out_specs,
        out_shape=out_shape,
        compiler_params=_params("parallel", "arbitrary"),
        name="banded_attention",
    )(*args)
    return res if with_lse else res[0]


def _dsa_kernel(qc_ref, qi_ref, kc_ref, ki_ref, vc_ref, wi_ref, o_ref,
                qsc_ref, qsi_ref, wt_ref, key_ref, acc_ref, *, k_sel):
    i = pl.program_id(1)
    n_heads = C_HEADS
    nch = i + 1
    wide = n_heads * LANES
    lane = lax.broadcasted_iota(jnp.int32, (1, LANES), 1)
    lo = lane < HEAD_DIM

    for h in range(n_heads):
        t = h // 2
        keep = lo if h % 2 == 0 else jnp.logical_not(lo)
        for src, dst in ((qc_ref, qsc_ref), (qi_ref, qsi_ref)):
            tile = src[:, t * LANES:(t + 1) * LANES].astype(F32)
            dst[h * LANES:(h + 1) * LANES, :] = jnp.where(keep, tile, 0.0).astype(BF16)
    wt_ref[...] = wi_ref[...].T

    row = lax.broadcasted_iota(jnp.int32, (BLOCK, LANES), 0)
    col = lax.broadcasted_iota(jnp.int32, (BLOCK, LANES), 1)

    def score_chunk(c, carry):
        k0 = pl.multiple_of(c * BLOCK, BLOCK)
        d = lax.dot_general(ki_ref[pl.ds(k0, BLOCK), :], qsi_ref[...],
                            (((1,), (1,)), ((), ())), preferred_element_type=F32)
        sc = jnp.zeros((BLOCK, LANES), F32)
        for h in range(IDX_HEADS):
            sc = sc + jnp.maximum(d[:, h * LANES:(h + 1) * LANES], 0.0) * wt_ref[h:h + 1, :]
        bits = pltpu.bitcast(sc, jnp.int32)
        key = bits ^ ((bits >> 31) & jnp.int32(0x7FFFFFFF))
        key = jnp.where(sc == 0.0, 0, key)
        causal = (k0 + row) <= (i * BLOCK + col)
        key_ref[pl.ds(k0, BLOCK), :] = jnp.where(causal, key, INT_MIN)
        return carry

    lax.fori_loop(0, nch, score_chunk, 0)

    def count(pred_fn):
        def body(c, acc):
            k0 = pl.multiple_of(c * BLOCK, BLOCK)
            hit = jnp.where(pred_fn(key_ref[pl.ds(k0, BLOCK), :]), 1, 0)
            return acc + jnp.sum(hit.reshape(BLOCK // 8, 8, LANES), axis=0)
        part = lax.fori_loop(0, nch, body, jnp.zeros((8, LANES), jnp.int32))
        return jnp.sum(part, axis=0, keepdims=True)

    def search():
        def bit_step(it, tu):
            cand_u = tu | lax.shift_left(jnp.int32(1), 31 - it)
            cand = cand_u ^ jnp.int32(INT_MIN)
            cnt = count(lambda k: k >= cand)
            return jnp.where(cnt >= k_sel, cand_u, tu)
        tu = lax.fori_loop(0, 32, bit_step, jnp.zeros((1, LANES), jnp.int32))
        return tu ^ jnp.int32(INT_MIN)

    thr = lax.cond(nch * BLOCK > k_sel, search, lambda: jnp.full((1, LANES), INT_MIN, jnp.int32))
    thr = jnp.maximum(thr, INT_MIN + 1)

    c_ge = count(lambda k: k >= thr)
    over = jnp.max(jnp.where(nch * BLOCK > k_sel, c_ge, 0)) > k_sel

    @pl.when(over)
    def _():
        need = (k_sel - count(lambda k: k > thr)).astype(F32)
        tri = (col <= row).astype(BF16)

        def body(c, carry):
            k0 = pl.multiple_of(c * BLOCK, BLOCK)
            key = key_ref[pl.ds(k0, BLOCK), :]
            eq = jnp.where(key == thr, 1.0, 0.0)
            pre = jnp.dot(tri, eq.astype(BF16), preferred_element_type=F32) + carry
            drop = (eq > 0.0) & (pre > need)
            key_ref[pl.ds(k0, BLOCK), :] = jnp.where(drop, INT_MIN, key)
            return carry + jnp.sum(eq, axis=0, keepdims=True)

        lax.fori_loop(0, nch, body, jnp.zeros((1, LANES), F32))

    acc_ref[...] = jnp.zeros_like(acc_ref)

    def att_chunk(c, carry):
        m, l = carry
        k0 = pl.multiple_of(c * BLOCK, BLOCK)
        s = lax.dot_general(kc_ref[pl.ds(k0, BLOCK), :], qsc_ref[...],
                            (((1,), (1,)), ((), ())), preferred_element_type=F32)
        sel = key_ref[pl.ds(k0, BLOCK), :] >= thr
        s = jnp.where(jnp.tile(sel, (1, n_heads)), s, -jnp.inf)
        m_new = jnp.maximum(m, jnp.max(s, axis=0, keepdims=True))
        m_safe = jnp.where(m_new == -jnp.inf, 0.0, m_new)
        alpha = jnp.exp(m - m_safe)
        p = jnp.exp(s - m_safe)
        l = l * alpha + jnp.sum(p, axis=0, keepdims=True)
        pv = lax.dot_general(vc_ref[pl.ds(k0, BLOCK), :], p.astype(BF16),
                             (((0,), (0,)), ((), ())), preferred_element_type=F32)
        acc_ref[...] = acc_ref[...] * alpha + pv
        return m_new, l

    m0 = jnp.full((1, wide), -jnp.inf, F32)
    l0 = jnp.zeros((1, wide), F32)
    _, l = lax.fori_loop(0, nch, att_chunk, (m0, l0))

    o_t = acc_ref[...] / l
    lo2 = lax.broadcasted_iota(jnp.int32, (BLOCK, LANES), 1) < HEAD_DIM
    for t in range(n_heads // 2):
        ev = o_t[:, (2 * t) * LANES:(2 * t + 1) * LANES].T
        od = o_t[:, (2 * t + 1) * LANES:(2 * t + 2) * LANES].T
        o_ref[:, t * LANES:(t + 1) * LANES] = jnp.where(lo2, ev, od).astype(o_ref.dtype)


def _dsa(c_pack, wi, batch, seq):
    n = batch * seq
    nq = seq // BLOCK
    k_sel = min(TOPK_MAX, seq // 4)
    qw = C_HEADS * HEAD_DIM
    qrow = lambda blk: pl.BlockSpec((BLOCK, qw), lambda b, i: (b * nq + i, blk))
    full = lambda col: pl.BlockSpec((seq, LANES), lambda b, i: (b, col // LANES))
    return pl.pallas_call(
        functools.partial(_dsa_kernel, k_sel=k_sel),
        grid=(batch, nq),
        in_specs=[qrow(0), qrow(1), full(C_KC), full(C_KI), full(C_VC),
                  pl.BlockSpec((BLOCK, LANES), lambda b, i: (b * nq + i, 0))],
        out_specs=pl.BlockSpec((BLOCK, qw), lambda b, i: (b * nq + i, 0)),
        out_shape=jax.ShapeDtypeStruct((n, qw), BF16),
        scratch_shapes=[
            pltpu.VMEM((C_HEADS * LANES, LANES), BF16),
            pltpu.VMEM((IDX_HEADS * LANES, LANES), BF16),
            pltpu.VMEM((LANES, LANES), F32),
            pltpu.VMEM((seq, LANES), jnp.int32),
            pltpu.VMEM((LANES, C_HEADS * LANES), F32),
        ],
        compiler_params=_params("parallel", "arbitrary"),
        name="dsa",
    )(c_pack, c_pack, c_pack, c_pack, c_pack, wi)


def _merge_kernel(x_ref, oa0_ref, oa1_ref, oa2_ref, l0_ref, l1_ref, l2_ref, ob_ref, oc_ref,
                  wg_ref, wa_ref, wb_ref, wc_ref, wo_ref, g_ref, b_ref, o_ref, *, alpha):
    x = x_ref[...]
    xb = x.astype(BF16)
    l0, l1, l2 = l0_ref[...], l1_ref[...], l2_ref[...]
    m = jnp.maximum(jnp.maximum(l0, l1), l2)
    e0, e1, e2 = jnp.exp(l0 - m), jnp.exp(l1 - m), jnp.exp(l2 - m)
    tot = e0 + e1 + e2
    oa = (e0 / tot) * oa0_ref[...] + (e1 / tot) * oa1_ref[...] + (e2 / tot) * oa2_ref[...]
    d = x.shape[1]

    def branch(o, w_ref, k):
        gate = jnp.dot(xb, wg_ref[:, k * d:(k + 1) * d], preferred_element_type=F32)
        return jax.nn.sigmoid(gate) * jnp.dot(o, w_ref[...], preferred_element_type=F32)

    merged = (branch(oa.astype(BF16), wa_ref, 0) + branch(ob_ref[...], wb_ref, 1)
              + branch(oc_ref[...], wc_ref, 2))
    out = jnp.dot(merged.astype(BF16), wo_ref[...], preferred_element_type=F32)
    o_ref[...] = _layer_norm(alpha * x + out, g_ref[...], b_ref[...])


def _merge(x, oa, lse, ob, oc, w_gates, w_a, w_b, w_c, w_o, g, b, alpha):
    n, d = x.shape
    tm = 256
    bw = oa[0].shape[1]
    row = lambda w: pl.BlockSpec((tm, w), lambda i: (i, 0))
    const = lambda a: pl.BlockSpec(a.shape, lambda i: (0, 0))
    return pl.pallas_call(
        functools.partial(_merge_kernel, alpha=alpha),
        grid=(n // tm,),
        in_specs=[row(d)] + [row(bw)] * 8
        + [const(w_gates), const(w_a), const(w_b), const(w_c), const(w_o), const(g), const(b)],
        out_specs=row(d),
        out_shape=jax.ShapeDtypeStruct((n, d), F32),
        compiler_params=_params("parallel"),
        name="merge_ln",
    )(x, *oa, *lse, ob, oc, w_gates, w_a, w_b, w_c, w_o, g, b)


def _mixer(x, cos, sin, w_in, sink_b, w_br_a, w_br_b, w_br_c, w_out, g, b, alpha, batch, seq):
    n = batch * seq
    w_packed, w_gates = _pack_in_weights(w_in)
    a_pack, b_pack, c_pack, wi = _inproj(x, w_packed, cos, sin)

    oa, lse = [], []
    for window, dil in A_CONFIGS:
        t = a_pack.reshape(batch, seq // dil, dil, PACK_A)
        t = t.transpose(0, 2, 1, 3).reshape(batch * dil, seq // dil, PACK_A)
        o, ls = _banded_attention(t, q_col=0, k_col=A_W, v_col=2 * A_W, q_width=A_W, kv_width=A_W,
                                  kv_tile_of=lambda tt: tt, max_dist=window // dil,
                                  with_lse=True, out_dtype=F32)
        unfold = lambda u: u.reshape(batch, dil, seq // dil, A_W).transpose(0, 2, 1, 3).reshape(n, A_W)
        oa.append(unfold(o))
        lse.append(unfold(ls))

    ob = _banded_attention(b_pack.reshape(batch, seq, PACK_B), q_col=0, k_col=512, v_col=768,
                           q_width=512, kv_width=256, kv_tile_of=lambda tt: tt // 2,
                           max_dist=B_WINDOW - 1, sink=sink_b).reshape(n, 512)

    oc = _dsa(c_pack, wi, batch, seq)

    return _merge(x, oa, lse, ob, oc, w_gates, w_br_a.astype(BF16), w_br_b.astype(BF16),
                  w_br_c.astype(BF16), w_out.astype(BF16), g, b, alpha)


def kernel(x, positions, w_in, sink_b, w_br_a, w_br_b, w_br_c, w_out,
           ffn1_in, ffn1_out, ffn2_in, ffn2_out, ln_g, ln_b):
    batch, seq, d = x.shape
    depth = w_in.shape[0]
    alpha = float((2 * depth) ** 0.25)
    cos, sin = _rope_tables(positions)
    h = x.reshape(batch * seq, d)
    for l in range(depth):
        gb = lambda k: (ln_g[l, k][None, :], ln_b[l, k][None, :])
        h = _ffn(h, ffn1_in[l].astype(BF16), ffn1_out[l].astype(BF16), *gb(0), alpha)
        h = _mixer(h, cos, sin, w_in[l], sink_b[l], w_br_a[l], w_br_b[l], w_br_c[l], w_out[l],
                   *gb(1), alpha, batch, seq)
        h = _ffn(h, ffn2_in[l].astype(BF16), ffn2_out[l].astype(BF16), *gb(2), alpha)
    return h.reshape(batch, seq, d)
```

```python
import functools

import jax
import jax.numpy as jnp
import numpy as np
from jax import lax
from jax.experimental import pallas as pl
from jax.experimental.pallas import tpu as pltpu

HEAD_DIM = 64
ROPE_THETA = 10000.0
A_HEADS = 8
A_CONFIGS = ((128, 1), (512, 4), (2048, 16))
B_HEADS = 8
B_KV_HEADS = 2
B_WINDOW = 128
C_HEADS = 8
IDX_HEADS = 8
TOPK_MAX = 256
BLOCK = 128
LN_EPS = 1e-5
LANES = 128
VMEM_LIMIT_BYTES = 56 * 1024 * 1024
INT_MIN = -(2 ** 31)
I16_MIN = -(2 ** 15)

F32 = jnp.float32
BF16 = jnp.bfloat16
I16 = jnp.int16
I32 = jnp.int32


def _params(*sem):
    return pltpu.CompilerParams(dimension_semantics=sem, vmem_limit_bytes=VMEM_LIMIT_BYTES)


def _layer_norm(y, g, b):
    mu = jnp.mean(y, axis=-1, keepdims=True)
    d = y - mu
    var = jnp.mean(d * d, axis=-1, keepdims=True)
    return d * lax.rsqrt(var + LN_EPS) * g + b


def _rope_table_kernel(pos_ref, inv_ref, sign_ref, cos_ref, sin_ref):
    ang = pos_ref[...].astype(F32) * inv_ref[...]
    cos_ref[...] = jnp.cos(ang)
    sin_ref[...] = jnp.sin(ang) * sign_ref[...]


def _rope_tables(positions):
    n = positions.size
    tm = 1024
    inv = ROPE_THETA ** (-jnp.arange(0, HEAD_DIM, 2, dtype=F32) / HEAD_DIM)
    inv = jnp.tile(inv, LANES // inv.shape[0])[None, :]
    half = (np.arange(LANES) % HEAD_DIM) < HEAD_DIM // 2
    sign = jnp.asarray(np.where(half, -1.0, 1.0), F32)[None, :]
    row = pl.BlockSpec((tm, LANES), lambda i: (i, 0))
    const = pl.BlockSpec((1, LANES), lambda i: (0, 0))
    return pl.pallas_call(
        _rope_table_kernel,
        grid=(n // tm,),
        in_specs=[pl.BlockSpec((tm, 1), lambda i: (i, 0)), const, const],
        out_specs=[row, row],
        out_shape=[jax.ShapeDtypeStruct((n, LANES), F32)] * 2,
        compiler_params=_params("parallel"),
        name="rope_tables",
    )(positions.reshape(n, 1), inv, sign)


def _ffn_kernel(x_ref, wg_ref, wu_ref, wo_ref, g_ref, b_ref, o_ref, acc_ref, *, alpha, nf):
    j = pl.program_id(1)
    xb = x_ref[...].astype(BF16)
    gate = jnp.dot(xb, wg_ref[...], preferred_element_type=F32)
    up = jnp.dot(xb, wu_ref[...], preferred_element_type=F32)
    h = (gate * jax.nn.sigmoid(gate) * up).astype(BF16)
    part = jnp.dot(h, wo_ref[...], preferred_element_type=F32)

    @pl.when(j == 0)
    def _():
        acc_ref[...] = part

    @pl.when(j > 0)
    def _():
        acc_ref[...] += part

    @pl.when(j == nf - 1)
    def _():
        y = alpha * x_ref[...] + 0.5 * acc_ref[...]
        o_ref[...] = _layer_norm(y, g_ref[...], b_ref[...])


def _ffn(x, w_in, w_out, g, b, alpha):
    n, d = x.shape
    d_ff = w_out.shape[0]
    tm = 512
    nf = 2
    tf = d_ff // nf
    return pl.pallas_call(
        functools.partial(_ffn_kernel, alpha=alpha, nf=nf),
        grid=(n // tm, nf),
        in_specs=[
            pl.BlockSpec((tm, d), lambda i, j: (i, 0)),
            pl.BlockSpec((d, tf), lambda i, j: (0, j)),
            pl.BlockSpec((d, tf), lambda i, j: (0, nf + j)),
            pl.BlockSpec((tf, d), lambda i, j: (j, 0)),
            pl.BlockSpec((1, d), lambda i, j: (0, 0)),
            pl.BlockSpec((1, d), lambda i, j: (0, 0)),
        ],
        out_specs=pl.BlockSpec((tm, d), lambda i, j: (i, 0)),
        out_shape=jax.ShapeDtypeStruct((n, d), F32),
        scratch_shapes=[pltpu.VMEM((tm, d), F32)],
        compiler_params=_params("parallel", "arbitrary"),
        name="ffn_ln",
    )(x, w_in, w_in, w_out, g, b)


A_W = A_HEADS * HEAD_DIM
PACK_A = 3 * A_W
PACK_B = B_HEADS * HEAD_DIM + 4 * LANES
PACK_C = 2 * C_HEADS * HEAD_DIM + 3 * LANES
C_KC, C_KI, C_VC = 1024, 1152, 1280


def _pack_in_weights(w):
    sizes = (A_W, A_W, A_W, 512, 128, 128, 512, 64, 64, 512, 64, 8, 1024, 1024, 1024)
    offs = np.concatenate([[0], np.cumsum(sizes)])
    (qa, ka, va, qb, kb, vb, qc, kc, vc, qi, ki, wi, ga, gb, gc) = [
        w[:, offs[i]:offs[i + 1]] for i in range(len(sizes))]

    def dup_heads(t):
        g = t.shape[1] // HEAD_DIM
        t = t.reshape(t.shape[0], g, 1, HEAD_DIM)
        return jnp.broadcast_to(t, (t.shape[0], g, 2, HEAD_DIM)).reshape(t.shape[0], g * LANES)

    wi_pad = jnp.pad(wi, ((0, 0), (0, LANES - wi.shape[1])))
    packed = jnp.concatenate(
        [qa, ka, va, qb, dup_heads(kb), dup_heads(vb),
         qc, qi, dup_heads(kc), dup_heads(ki), dup_heads(vc), wi_pad], axis=1)
    gates = jnp.concatenate([ga, gb, gc], axis=1)
    return packed.astype(BF16), gates.astype(BF16)


def _rope(h, cos, sin):
    w = h.shape[1]
    reps = w // LANES
    c = jnp.tile(cos, (1, reps)) if reps > 1 else cos
    s = jnp.tile(sin, (1, reps)) if reps > 1 else sin
    lane = lax.broadcasted_iota(jnp.int32, h.shape, 1)
    first = (lane % HEAD_DIM) < HEAD_DIM // 2
    partner = jnp.where(first, pltpu.roll(h, w - HEAD_DIM // 2, 1), pltpu.roll(h, HEAD_DIM // 2, 1))
    return h * c + partner * s


def _inproj_kernel(x_ref, w_ref, cos_ref, sin_ref, a_ref, b_ref, c_ref, wi_ref):
    xb = x_ref[...].astype(BF16)
    cos = cos_ref[...]
    sin = sin_ref[...]
    scale = HEAD_DIM ** -0.5

    def proj(c0, width):
        return jnp.dot(xb, w_ref[:, c0:c0 + width], preferred_element_type=F32)

    a_ref[:, 0:512] = (_rope(proj(0, 512), cos, sin) * scale).astype(BF16)
    a_ref[:, 512:1024] = _rope(proj(512, 512), cos, sin).astype(BF16)
    a_ref[:, 1024:1536] = proj(1024, 512).astype(BF16)
    o = PACK_A
    b_ref[:, 0:512] = (_rope(proj(o, 512), cos, sin) * scale).astype(BF16)
    b_ref[:, 512:768] = _rope(proj(o + 512, 256), cos, sin).astype(BF16)
    b_ref[:, 768:1024] = proj(o + 768, 256).astype(BF16)
    o = PACK_A + PACK_B
    c_ref[:, 0:512] = (_rope(proj(o, 512), cos, sin) * scale).astype(BF16)
    c_ref[:, 512:1024] = _rope(proj(o + 512, 512), cos, sin).astype(BF16)
    c_ref[:, 1024:1280] = _rope(proj(o + 1024, 256), cos, sin).astype(BF16)
    c_ref[:, 1280:1408] = proj(o + 1280, 128).astype(BF16)
    wi_ref[...] = proj(o + 1408, 128)


def _inproj(x, w_packed, cos, sin):
    n, d = x.shape
    tm = 512
    wtot = w_packed.shape[1]
    row = lambda w: pl.BlockSpec((tm, w), lambda i: (i, 0))
    return pl.pallas_call(
        _inproj_kernel,
        grid=(n // tm,),
        in_specs=[row(d), pl.BlockSpec((d, wtot), lambda i: (0, 0)), row(LANES), row(LANES)],
        out_specs=[row(PACK_A), row(PACK_B), row(PACK_C), row(LANES)],
        out_shape=[
            jax.ShapeDtypeStruct((n, PACK_A), BF16),
            jax.ShapeDtypeStruct((n, PACK_B), BF16),
            jax.ShapeDtypeStruct((n, PACK_C), BF16),
            jax.ShapeDtypeStruct((n, LANES), F32),
        ],
        compiler_params=_params("parallel"),
        name="inproj_rope",
    )(x, w_packed, cos, sin)


def _banded_kernel(*refs, n_tiles, kv_tile_of, max_dist, has_sink, with_lse):
    q_ref, kp_ref, kc_ref, vp_ref, vc_ref = refs[:5]
    rest = refs[5:]
    if has_sink:
        sink_ref, rest = rest[0], rest[1:]
    o_ref = rest[0]
    lse_ref = rest[1] if with_lse else None

    j = pl.program_id(1)
    qi = lax.broadcasted_iota(jnp.int32, (BLOCK, 2 * BLOCK), 0)
    kj = lax.broadcasted_iota(jnp.int32, (BLOCK, 2 * BLOCK), 1)
    dist = qi + BLOCK - kj
    valid = (dist >= 0) & (dist <= max_dist) & ((kj >= BLOCK) | (j > 0))
    lane = lax.broadcasted_iota(jnp.int32, (1, LANES), 1)
    lo = lane < HEAD_DIM

    for t in range(n_tiles):
        g = kv_tile_of(t)
        q = q_ref[0, :, t * LANES:(t + 1) * LANES].astype(F32)
        k2 = jnp.concatenate([kp_ref[0, :, g * LANES:(g + 1) * LANES],
                              kc_ref[0, :, g * LANES:(g + 1) * LANES]], axis=0)
        v2 = jnp.concatenate([vp_ref[0, :, g * LANES:(g + 1) * LANES],
                              vc_ref[0, :, g * LANES:(g + 1) * LANES]], axis=0).astype(F32)
        acc = None
        dens = []
        lses = []
        for par in range(2):
            keep = lo if par == 0 else jnp.logical_not(lo)
            qm = jnp.where(keep, q, 0.0).astype(BF16)
            s = lax.dot_general(qm, k2, (((1,), (1,)), ((), ())), preferred_element_type=F32)
            s = jnp.where(valid, s, -jnp.inf)
            m = jnp.max(s, axis=-1, keepdims=True)
            if has_sink:
                h = 2 * t + par
                sk = sink_ref[0:1, h:h + 1]
                m = jnp.maximum(m, sk)
            p = jnp.exp(s - m)
            den = jnp.sum(p, axis=-1, keepdims=True)
            if has_sink:
                den = den + jnp.exp(sk - m)
            vm = jnp.where(keep, v2, 0.0).astype(BF16)
            o = jnp.dot(p.astype(BF16), vm, preferred_element_type=F32)
            acc = o if acc is None else acc + o
            dens.append(den)
            lses.append(m + jnp.log(den))
        den2 = jnp.where(lo, dens[0], dens[1])
        o_ref[0, :, t * LANES:(t + 1) * LANES] = (acc / den2).astype(o_ref.dtype)
        if with_lse:
            lse_ref[0, :, t * LANES:(t + 1) * LANES] = jnp.where(lo, lses[0], lses[1])


def _banded_attention(qkv, *, q_col, k_col, v_col, q_width, kv_width, kv_tile_of, max_dist,
                      sink=None, with_lse=False, out_dtype=BF16):
    n, l, _ = qkv.shape
    nb = l // BLOCK
    n_tiles = q_width // LANES
    qs = pl.BlockSpec((1, BLOCK, q_width), lambda b, j: (b, j, q_col // q_width))
    cur = lambda col: pl.BlockSpec((1, BLOCK, kv_width), lambda b, j: (b, j, col // kv_width))
    prev = lambda col: pl.BlockSpec((1, BLOCK, kv_width),
                                    lambda b, j: (b, jnp.maximum(j - 1, 0), col // kv_width))
    in_specs = [qs, prev(k_col), cur(k_col), prev(v_col), cur(v_col)]
    args = [qkv] * 5
    if sink is not None:
        in_specs.append(pl.BlockSpec((1, sink.shape[-1]), lambda b, j: (0, 0)))
        args.append(sink.reshape(1, -1).astype(F32))
    os = pl.BlockSpec((1, BLOCK, q_width), lambda b, j: (b, j, 0))
    out_specs = [os]
    out_shape = [jax.ShapeDtypeStruct((n, l, q_width), out_dtype)]
    if with_lse:
        out_specs.append(os)
        out_shape.append(jax.ShapeDtypeStruct((n, l, q_width), F32))
    res = pl.pallas_call(
        functools.partial(_banded_kernel, n_tiles=n_tiles, kv_tile_of=kv_tile_of,
                          max_dist=max_dist, has_sink=sink is not None, with_lse=with_lse),
        grid=(n, nb),
        in_specs=in_specs,
        out_specs=out_specs,
        out_shape=out_shape,
        compiler_params=_params("parallel", "arbitrary"),
        name="banded_attention",
    )(*args)
    return res if with_lse else res[0]


DSA_CHUNK = 2 * BLOCK


def _fold_rows(x, rows):
    parts = [x[r:r + rows] for r in range(0, x.shape[0], rows)]
    while len(parts) > 1:
        parts = [a + b for a, b in zip(parts[::2], parts[1::2])] + (parts[-1:] if len(parts) % 2 else [])
    return parts[0]


def _dsa_kernel(qc0_ref, qc1_ref, qi0_ref, qi1_ref, kc_ref, ki_ref, vc_ref, wi0_ref, wi1_ref, o_ref,
                qsc_ref, qsi_ref, wt_ref, key_ref, hi_ref, lo_ref, acc_ref, *, k_sel, n_tiles):
    j = pl.program_id(1)
    n_heads = C_HEADS
    ch = DSA_CHUNK
    n_chunks = (n_tiles * BLOCK) // ch + 1
    wide = n_heads * LANES
    tile_of = (j, n_tiles - 1 - j)
    n0 = (j * BLOCK + BLOCK + ch - 1) // ch
    lane = lax.broadcasted_iota(I32, (1, LANES), 1)
    lo_lanes = lane < HEAD_DIM

    for slot, (qc_ref, qi_ref, wi_ref) in enumerate(((qc0_ref, qi0_ref, wi0_ref),
                                                     (qc1_ref, qi1_ref, wi1_ref))):
        for h in range(n_heads):
            t = h // 2
            keep = lo_lanes if h % 2 == 0 else jnp.logical_not(lo_lanes)
            for src, dst in ((qc_ref, qsc_ref), (qi_ref, qsi_ref)):
                tile = src[:, t * LANES:(t + 1) * LANES].astype(F32)
                dst[slot, h * LANES:(h + 1) * LANES, :] = jnp.where(keep, tile, 0.0).astype(BF16)
        wt_ref[slot] = wi_ref[...].T

    def chunk_info(c):
        is1 = c >= n0
        slot = jnp.where(is1, 1, 0)
        k0 = pl.multiple_of((c - jnp.where(is1, n0, 0)) * ch, ch)
        q0 = jnp.where(is1, tile_of[1], tile_of[0]) * BLOCK
        return is1, slot, k0, q0

    info = [chunk_info(c) for c in range(n_chunks)]
    pick = lambda is1, a: jnp.where(is1, a[1], a[0])

    row = lax.broadcasted_iota(I32, (ch, LANES), 0)
    col = lax.broadcasted_iota(I32, (ch, LANES), 1)

    for c, (is1, slot, k0, q0) in enumerate(info):
        d = lax.dot_general(ki_ref[pl.ds(k0, ch), :], qsi_ref[slot],
                            (((1,), (1,)), ((), ())), preferred_element_type=F32)
        sc = jnp.zeros((ch, LANES), F32)
        for h in range(IDX_HEADS):
            sc = sc + jnp.maximum(d[:, h * LANES:(h + 1) * LANES], 0.0) * wt_ref[slot, h:h + 1, :]
        bits = pltpu.bitcast(sc, I32)
        key = bits ^ ((bits >> 31) & jnp.int32(0x7FFFFFFF))
        key = jnp.where(sc == 0.0, 0, key)
        key = jnp.where((k0 + row) <= (q0 + col), key, INT_MIN)
        key_ref[c] = key
        hi_ref[c] = (key >> 16).astype(I16)
        lo_ref[c] = ((key & 0xFFFF) - 32768).astype(I16)

    def count16(ref, cands, strict=False):
        acc = [jnp.zeros((16, LANES), I16), jnp.zeros((16, LANES), I16)]
        for c, (is1, _, _, _) in enumerate(info):
            cand = pick(is1, cands)
            blk = ref[c]
            hit = jnp.where(blk > cand if strict else blk >= cand, jnp.int16(1), jnp.int16(0))
            part = _fold_rows(hit, 16)
            zero = jnp.zeros_like(part)
            acc[0] = acc[0] + jnp.where(is1, zero, part)
            acc[1] = acc[1] + jnp.where(is1, part, zero)
        return [jnp.sum(a.astype(I32), axis=0, keepdims=True) for a in acc]

    def search16(ref, need):
        def step(it, tu):
            bit = lax.shift_left(jnp.int32(1), 15 - it)
            cu = [tu[0] | bit, tu[1] | bit]
            cnt = count16(ref, [(u - 32768).astype(I16) for u in cu])
            return tuple(jnp.where(cnt[s] >= need[s], cu[s], tu[s]) for s in range(2))
        z = jnp.zeros((1, LANES), I32)
        tu = lax.fori_loop(0, 16, step, (z, z))
        return [u - 32768 for u in tu]

    ksel = jnp.full((1, LANES), k_sel, I32)
    p1 = search16(hi_ref, [ksel, ksel])
    p1_16 = [p.astype(I16) for p in p1]
    above = count16(hi_ref, p1_16, strict=True)
    for c, (is1, _, _, _) in enumerate(info):
        lo_ref[c] = jnp.where(hi_ref[c] == pick(is1, p1_16), lo_ref[c], jnp.int16(I16_MIN))
    p2 = search16(lo_ref, [ksel - above[0], ksel - above[1]])
    thr = [jnp.maximum((p1[s] << 16) | ((p2[s] + 32768) & 0xFFFF), INT_MIN + 1) for s in range(2)]

    def count32(pred):
        acc = [jnp.zeros((8, LANES), I32), jnp.zeros((8, LANES), I32)]
        for c, (is1, _, _, _) in enumerate(info):
            hit = jnp.where(pred(key_ref[c], pick(is1, thr)), 1, 0)
            part = jnp.sum(hit.reshape(ch // 8, 8, LANES), axis=0)
            zero = jnp.zeros_like(part)
            acc[0] = acc[0] + jnp.where(is1, zero, part)
            acc[1] = acc[1] + jnp.where(is1, part, zero)
        return [jnp.sum(a, axis=0, keepdims=True) for a in acc]

    c_ge = count32(lambda k, t: k >= t)
    over = jnp.max(jnp.maximum(c_ge[0], c_ge[1])) > k_sel

    @pl.when(over)
    def _():
        c_gt = count32(lambda k, t: k > t)
        need = [(k_sel - c).astype(F32) for c in c_gt]
        tri = (lax.broadcasted_iota(I32, (ch, ch), 1)
               <= lax.broadcasted_iota(I32, (ch, ch), 0)).astype(BF16)
        carry = jnp.zeros((1, LANES), F32)
        for c, (is1, _, _, _) in enumerate(info):
            carry = jnp.where(n0 == c, 0.0, carry)
            key = key_ref[c]
            eq = jnp.where(key == pick(is1, thr), 1.0, 0.0)
            pre = jnp.dot(tri, eq.astype(BF16), preferred_element_type=F32) + carry
            drop = (eq > 0.0) & (pre > pick(is1, need))
            key_ref[c] = jnp.where(drop, INT_MIN, key)
            carry = carry + jnp.sum(eq, axis=0, keepdims=True)

    acc_ref[...] = jnp.zeros_like(acc_ref)
    m = [jnp.full((1, wide), -jnp.inf, F32)] * 2
    l = [jnp.zeros((1, wide), F32)] * 2
    for c, (is1, slot, k0, _) in enumerate(info):
        s = lax.dot_general(kc_ref[pl.ds(k0, ch), :], qsc_ref[slot],
                            (((1,), (1,)), ((), ())), preferred_element_type=F32)
        sel = key_ref[c] >= pick(is1, thr)
        s = jnp.where(jnp.tile(sel, (1, n_heads)), s, -jnp.inf)
        m_old = pick(is1, m)
        m_new = jnp.maximum(m_old, jnp.max(s, axis=0, keepdims=True))
        m_safe = jnp.where(m_new == -jnp.inf, 0.0, m_new)
        alpha = jnp.exp(m_old - m_safe)
        p = jnp.exp(s - m_safe)
        l_new = pick(is1, l) * alpha + jnp.sum(p, axis=0, keepdims=True)
        pv = lax.dot_general(vc_ref[pl.ds(k0, ch), :], p.astype(BF16),
                             (((0,), (0,)), ((), ())), preferred_element_type=F32)
        acc_ref[slot] = acc_ref[slot] * alpha + pv
        m = [jnp.where(is1, m[0], m_new), jnp.where(is1, m_new, m[1])]
        l = [jnp.where(is1, l[0], l_new), jnp.where(is1, l_new, l[1])]

    lo2 = lax.broadcasted_iota(I32, (BLOCK, LANES), 1) < HEAD_DIM
    for slot in range(2):
        o_t = acc_ref[slot] / l[slot]
        for t in range(n_heads // 2):
            ev = o_t[:, (2 * t) * LANES:(2 * t + 1) * LANES].T
            od = o_t[:, (2 * t + 1) * LANES:(2 * t + 2) * LANES].T
            o_ref[0, tile_of[slot], :, t * LANES:(t + 1) * LANES] = (
                jnp.where(lo2, ev, od).astype(o_ref.dtype))


def _dsa(c_pack, wi, batch, seq):
    n = batch * seq
    nq = seq // BLOCK
    assert nq % 2 == 0 and (nq * BLOCK) % DSA_CHUNK == 0
    n_chunks = (nq * BLOCK) // DSA_CHUNK + 1
    k_sel = min(TOPK_MAX, seq // 4)
    qw = C_HEADS * HEAD_DIM
    first = lambda b, j: b * nq + j
    second = lambda b, j: b * nq + nq - 1 - j
    qrow = lambda blk, which: pl.BlockSpec((BLOCK, qw), lambda b, j: (which(b, j), blk))
    wrow = lambda which: pl.BlockSpec((BLOCK, LANES), lambda b, j: (which(b, j), 0))
    full = lambda col: pl.BlockSpec((seq, LANES), lambda b, j: (b, col // LANES))
    out = pl.pallas_call(
        functools.partial(_dsa_kernel, k_sel=k_sel, n_tiles=nq),
        grid=(batch, nq // 2),
        in_specs=[qrow(0, first), qrow(0, second), qrow(1, first), qrow(1, second),
                  full(C_KC), full(C_KI), full(C_VC), wrow(first), wrow(second)],
        out_specs=pl.BlockSpec((1, nq, BLOCK, qw), lambda b, j: (b, 0, 0, 0)),
        out_shape=jax.ShapeDtypeStruct((batch, nq, BLOCK, qw), BF16),
        scratch_shapes=[
            pltpu.VMEM((2, C_HEADS * LANES, LANES), BF16),
            pltpu.VMEM((2, IDX_HEADS * LANES, LANES), BF16),
            pltpu.VMEM((2, LANES, LANES), F32),
            pltpu.VMEM((n_chunks, DSA_CHUNK, LANES), I32),
            pltpu.VMEM((n_chunks, DSA_CHUNK, LANES), I16),
            pltpu.VMEM((n_chunks, DSA_CHUNK, LANES), I16),
            pltpu.VMEM((2, LANES, C_HEADS * LANES), F32),
        ],
        compiler_params=_params("parallel", "arbitrary"),
        name="dsa",
    )(c_pack, c_pack, c_pack, c_pack, c_pack, c_pack, c_pack, wi, wi)
    return out.reshape(n, qw)


def _merge_kernel(x_ref, oa0_ref, oa1_ref, oa2_ref, l0_ref, l1_ref, l2_ref, ob_ref, oc_ref,
                  wg_ref, wa_ref, wb_ref, wc_ref, wo_ref, g_ref, b_ref, o_ref, *, alpha):
    x = x_ref[...]
    xb = x.astype(BF16)
    l0, l1, l2 = l0_ref[...], l1_ref[...], l2_ref[...]
    m = jnp.maximum(jnp.maximum(l0, l1), l2)
    e0, e1, e2 = jnp.exp(l0 - m), jnp.exp(l1 - m), jnp.exp(l2 - m)
    tot = e0 + e1 + e2
    oa = (e0 / tot) * oa0_ref[...] + (e1 / tot) * oa1_ref[...] + (e2 / tot) * oa2_ref[...]
    d = x.shape[1]

    def branch(o, w_ref, k):
        gate = jnp.dot(xb, wg_ref[:, k * d:(k + 1) * d], preferred_element_type=F32)
        return jax.nn.sigmoid(gate) * jnp.dot(o, w_ref[...], preferred_element_type=F32)

    merged = (branch(oa.astype(BF16), wa_ref, 0) + branch(ob_ref[...], wb_ref, 1)
              + branch(oc_ref[...], wc_ref, 2))
    out = jnp.dot(merged.astype(BF16), wo_ref[...], preferred_element_type=F32)
    o_ref[...] = _layer_norm(alpha * x + out, g_ref[...], b_ref[...])


def _merge(x, oa, lse, ob, oc, w_gates, w_a, w_b, w_c, w_o, g, b, alpha):
    n, d = x.shape
    tm = 256
    bw = oa[0].shape[1]
    row = lambda w: pl.BlockSpec((tm, w), lambda i: (i, 0))
    const = lambda a: pl.BlockSpec(a.shape, lambda i: (0, 0))
    return pl.pallas_call(
        functools.partial(_merge_kernel, alpha=alpha),
        grid=(n // tm,),
        in_specs=[row(d)] + [row(bw)] * 8
        + [const(w_gates), const(w_a), const(w_b), const(w_c), const(w_o), const(g), const(b)],
        out_specs=row(d),
        out_shape=jax.ShapeDtypeStruct((n, d), F32),
        compiler_params=_params("parallel"),
        name="merge_ln",
    )(x, *oa, *lse, ob, oc, w_gates, w_a, w_b, w_c, w_o, g, b)


def _mixer(x, cos, sin, w_in, sink_b, w_br_a, w_br_b, w_br_c, w_out, g, b, alpha, batch, seq):
    n = batch * seq
    w_packed, w_gates = _pack_in_weights(w_in)
    a_pack, b_pack, c_pack, wi = _inproj(x, w_packed, cos, sin)

    oa, lse = [], []
    for window, dil in A_CONFIGS:
        t = a_pack.reshape(batch, seq // dil, dil, PACK_A)
        t = t.transpose(0, 2, 1, 3).reshape(batch * dil, seq // dil, PACK_A)
        o, ls = _banded_attention(t, q_col=0, k_col=A_W, v_col=2 * A_W, q_width=A_W, kv_width=A_W,
                                  kv_tile_of=lambda tt: tt, max_dist=window // dil,
                                  with_lse=True, out_dtype=F32)
        unfold = lambda u: u.reshape(batch, dil, seq // dil, A_W).transpose(0, 2, 1, 3).reshape(n, A_W)
        oa.append(unfold(o))
        lse.append(unfold(ls))

    ob = _banded_attention(b_pack.reshape(batch, seq, PACK_B), q_col=0, k_col=512, v_col=768,
                           q_width=512, kv_width=256, kv_tile_of=lambda tt: tt // 2,
                           max_dist=B_WINDOW - 1, sink=sink_b).reshape(n, 512)

    oc = _dsa(c_pack, wi, batch, seq)

    return _merge(x, oa, lse, ob, oc, w_gates, w_br_a.astype(BF16), w_br_b.astype(BF16),
                  w_br_c.astype(BF16), w_out.astype(BF16), g, b, alpha)


def kernel(x, positions, w_in, sink_b, w_br_a, w_br_b, w_br_c, w_out,
           ffn1_in, ffn1_out, ffn2_in, ffn2_out, ln_g, ln_b):
    batch, seq, d = x.shape
    depth = w_in.shape[0]
    alpha = float((2 * depth) ** 0.25)
    cos, sin = _rope_tables(positions)
    h = x.reshape(batch * seq, d)
    for l in range(depth):
        gb = lambda k: (ln_g[l, k][None, :], ln_b[l, k][None, :])
        h = _ffn(h, ffn1_in[l].astype(BF16), ffn1_out[l].astype(BF16), *gb(0), alpha)
        h = _mixer(h, cos, sin, w_in[l], sink_b[l], w_br_a[l], w_br_b[l], w_br_c[l], w_out[l],
                   *gb(1), alpha, batch, seq)
        h = _ffn(h, ffn2_in[l].astype(BF16), ffn2_out[l].astype(BF16), *gb(2), alpha)
    return h.reshape(batch, seq, d)
```

```python
import functools

import jax
import jax.numpy as jnp
import numpy as np
from jax import lax
from jax.experimental import pallas as pl
from jax.experimental.pallas import tpu as pltpu

HEAD_DIM = 64
ROPE_THETA = 10000.0
A_HEADS = 8
A_CONFIGS = ((128, 1), (512, 4), (2048, 16))
B_HEADS = 8
B_KV_HEADS = 2
B_WINDOW = 128
C_HEADS = 8
IDX_HEADS = 8
TOPK_MAX = 256
BLOCK = 128
LN_EPS = 1e-5
LANES = 128
VMEM_LIMIT_BYTES = 56 * 1024 * 1024
INT_MIN = -(2 ** 31)
I16_MIN = -(2 ** 15)

F32 = jnp.float32
BF16 = jnp.bfloat16
I16 = jnp.int16
I32 = jnp.int32


def _params(*sem):
    return pltpu.CompilerParams(dimension_semantics=sem, vmem_limit_bytes=VMEM_LIMIT_BYTES)


def _layer_norm(y, g, b):
    mu = jnp.mean(y, axis=-1, keepdims=True)
    d = y - mu
    var = jnp.mean(d * d, axis=-1, keepdims=True)
    return d * lax.rsqrt(var + LN_EPS) * g + b


def _rope_table_kernel(pos_ref, inv_ref, sign_ref, cos_ref, sin_ref):
    ang = pos_ref[...].astype(F32) * inv_ref[...]
    cos_ref[...] = jnp.cos(ang)
    sin_ref[...] = jnp.sin(ang) * sign_ref[...]


def _rope_tables(positions):
    n = positions.size
    tm = 1024
    inv = ROPE_THETA ** (-jnp.arange(0, HEAD_DIM, 2, dtype=F32) / HEAD_DIM)
    inv = jnp.tile(inv, LANES // inv.shape[0])[None, :]
    half = (np.arange(LANES) % HEAD_DIM) < HEAD_DIM // 2
    sign = jnp.asarray(np.where(half, -1.0, 1.0), F32)[None, :]
    row = pl.BlockSpec((tm, LANES), lambda i: (i, 0))
    const = pl.BlockSpec((1, LANES), lambda i: (0, 0))
    return pl.pallas_call(
        _rope_table_kernel,
        grid=(n // tm,),
        in_specs=[pl.BlockSpec((tm, 1), lambda i: (i, 0)), const, const],
        out_specs=[row, row],
        out_shape=[jax.ShapeDtypeStruct((n, LANES), F32)] * 2,
        compiler_params=_params("parallel"),
        name="rope_tables",
    )(positions.reshape(n, 1), inv, sign)


def _ffn_kernel(x_ref, wg_ref, wu_ref, wo_ref, g_ref, b_ref, o_ref, acc_ref, *, alpha, nf):
    j = pl.program_id(1)
    xb = x_ref[...].astype(BF16)
    gate = jnp.dot(xb, wg_ref[...], preferred_element_type=F32)
    up = jnp.dot(xb, wu_ref[...], preferred_element_type=F32)
    h = (gate * jax.nn.sigmoid(gate) * up).astype(BF16)
    part = jnp.dot(h, wo_ref[...], preferred_element_type=F32)

    @pl.when(j == 0)
    def _():
        acc_ref[...] = part

    @pl.when(j > 0)
    def _():
        acc_ref[...] += part

    @pl.when(j == nf - 1)
    def _():
        y = alpha * x_ref[...] + 0.5 * acc_ref[...]
        o_ref[...] = _layer_norm(y, g_ref[...], b_ref[...])


def _ffn(x, w_in, w_out, g, b, alpha):
    n, d = x.shape
    d_ff = w_out.shape[0]
    tm = 512
    nf = 2
    tf = d_ff // nf
    return pl.pallas_call(
        functools.partial(_ffn_kernel, alpha=alpha, nf=nf),
        grid=(n // tm, nf),
        in_specs=[
            pl.BlockSpec((tm, d), lambda i, j: (i, 0)),
            pl.BlockSpec((d, tf), lambda i, j: (0, j)),
            pl.BlockSpec((d, tf), lambda i, j: (0, nf + j)),
            pl.BlockSpec((tf, d), lambda i, j: (j, 0)),
            pl.BlockSpec((1, d), lambda i, j: (0, 0)),
            pl.BlockSpec((1, d), lambda i, j: (0, 0)),
        ],
        out_specs=pl.BlockSpec((tm, d), lambda i, j: (i, 0)),
        out_shape=jax.ShapeDtypeStruct((n, d), F32),
        scratch_shapes=[pltpu.VMEM((tm, d), F32)],
        compiler_params=_params("parallel", "arbitrary"),
        name="ffn_ln",
    )(x, w_in, w_in, w_out, g, b)


A_W = A_HEADS * HEAD_DIM
PACK_A = 3 * A_W
PACK_B = B_HEADS * HEAD_DIM + 4 * LANES
PACK_C = 2 * C_HEADS * HEAD_DIM + 3 * LANES
C_KC, C_KI, C_VC = 1024, 1152, 1280


def _pack_in_weights(w):
    sizes = (A_W, A_W, A_W, 512, 128, 128, 512, 64, 64, 512, 64, 8, 1024, 1024, 1024)
    offs = np.concatenate([[0], np.cumsum(sizes)])
    (qa, ka, va, qb, kb, vb, qc, kc, vc, qi, ki, wi, ga, gb, gc) = [
        w[:, offs[i]:offs[i + 1]] for i in range(len(sizes))]

    def dup_heads(t):
        g = t.shape[1] // HEAD_DIM
        t = t.reshape(t.shape[0], g, 1, HEAD_DIM)
        return jnp.broadcast_to(t, (t.shape[0], g, 2, HEAD_DIM)).reshape(t.shape[0], g * LANES)

    wi_pad = jnp.pad(wi, ((0, 0), (0, LANES - wi.shape[1])))
    packed = jnp.concatenate(
        [qa, ka, va, qb, dup_heads(kb), dup_heads(vb),
         qc, qi, dup_heads(kc), dup_heads(ki), dup_heads(vc), wi_pad], axis=1)
    gates = jnp.concatenate([ga, gb, gc], axis=1)
    return packed.astype(BF16), gates.astype(BF16)


def _rope(h, cos, sin):
    w = h.shape[1]
    reps = w // LANES
    c = jnp.tile(cos, (1, reps)) if reps > 1 else cos
    s = jnp.tile(sin, (1, reps)) if reps > 1 else sin
    lane = lax.broadcasted_iota(jnp.int32, h.shape, 1)
    first = (lane % HEAD_DIM) < HEAD_DIM // 2
    partner = jnp.where(first, pltpu.roll(h, w - HEAD_DIM // 2, 1), pltpu.roll(h, HEAD_DIM // 2, 1))
    return h * c + partner * s


def _inproj_kernel(x_ref, w_ref, cos_ref, sin_ref, a_ref, b_ref, c_ref, wi_ref):
    xb = x_ref[...].astype(BF16)
    cos = cos_ref[...]
    sin = sin_ref[...]
    scale = HEAD_DIM ** -0.5

    def proj(c0, width):
        return jnp.dot(xb, w_ref[:, c0:c0 + width], preferred_element_type=F32)

    a_ref[:, 0:512] = _rope(proj(0, 512), cos, sin) * scale
    a_ref[:, 512:1024] = _rope(proj(512, 512), cos, sin)
    a_ref[:, 1024:1536] = proj(1024, 512)
    o = PACK_A
    b_ref[:, 0:512] = (_rope(proj(o, 512), cos, sin) * scale).astype(BF16)
    b_ref[:, 512:768] = _rope(proj(o + 512, 256), cos, sin).astype(BF16)
    b_ref[:, 768:1024] = proj(o + 768, 256).astype(BF16)
    o = PACK_A + PACK_B
    c_ref[:, 0:512] = (_rope(proj(o, 512), cos, sin) * scale).astype(BF16)
    c_ref[:, 512:1024] = _rope(proj(o + 512, 512), cos, sin).astype(BF16)
    c_ref[:, 1024:1280] = _rope(proj(o + 1024, 256), cos, sin).astype(BF16)
    c_ref[:, 1280:1408] = proj(o + 1280, 128).astype(BF16)
    wi_ref[...] = proj(o + 1408, 128)


def _inproj(x, w_packed, cos, sin):
    n, d = x.shape
    tm = 512
    wtot = w_packed.shape[1]
    row = lambda w: pl.BlockSpec((tm, w), lambda i: (i, 0))
    return pl.pallas_call(
        _inproj_kernel,
        grid=(n // tm,),
        in_specs=[row(d), pl.BlockSpec((d, wtot), lambda i: (0, 0)), row(LANES), row(LANES)],
        out_specs=[row(PACK_A), row(PACK_B), row(PACK_C), row(LANES)],
        out_shape=[
            jax.ShapeDtypeStruct((n, PACK_A), F32),
            jax.ShapeDtypeStruct((n, PACK_B), BF16),
            jax.ShapeDtypeStruct((n, PACK_C), BF16),
            jax.ShapeDtypeStruct((n, LANES), F32),
        ],
        compiler_params=_params("parallel"),
        name="inproj_rope",
    )(x, w_packed, cos, sin)


def _band_kernel(*refs, configs, seq, has_sink):
    q_ref, k_ref, v_ref = refs[:3]
    rest = refs[3:]
    if has_sink:
        sink_ref, rest = rest[0], rest[1:]
    o_ref, qs_ref, kf_ref, vf_ref = rest[:4]
    multi = len(configs) > 1
    if multi:
        num_ref, m_ref, den_ref = rest[4:7]

    lane = lax.broadcasted_iota(I32, (1, LANES), 1)
    lo = lane < HEAD_DIM
    if has_sink:
        sk = jnp.max(jnp.concatenate([jnp.broadcast_to(sink_ref[0, 0:1, :], (BLOCK, LANES)),
                                      jnp.broadcast_to(sink_ref[0, 1:2, :], (BLOCK, LANES))], axis=0),
                     axis=-1, keepdims=True)

    def valid_mask(nk, max_dist):
        qi = lax.broadcasted_iota(I32, (BLOCK, nk), 0)
        kj = lax.broadcasted_iota(I32, (BLOCK, nk), 1)
        dist = qi + (nk - BLOCK) - kj
        ok = (dist >= 0) & (dist <= max_dist)
        return jnp.concatenate([ok, ok], axis=0)

    def rows(ref, start, size, stride):
        if stride == 1:
            return ref[0, start:start + size, :]
        return ref[0, pl.ds(start, size, stride=stride), :]

    for c, (dil, max_dist) in enumerate(configs):
        fl = seq // dil
        for r in range(dil):
            dst = slice(r * fl, (r + 1) * fl)
            x = rows(q_ref, r, fl, dil).astype(F32)
            qs_ref[0, dst, :] = jnp.where(lo, x, 0.0).astype(BF16)
            qs_ref[1, dst, :] = jnp.where(lo, 0.0, x).astype(BF16)
            kf_ref[dst, :] = rows(k_ref, r, fl, dil).astype(BF16)
            vf_ref[dst, :] = rows(v_ref, r, fl, dil).astype(BF16)
        valid_full = valid_mask(2 * BLOCK, max_dist)
        valid_first = valid_mask(BLOCK, max_dist)
        for r in range(dil):
            for jb in range(fl // BLOCK):
                r0 = r * fl + jb * BLOCK
                k0 = r0 - BLOCK if jb > 0 else r0
                valid = valid_full if jb > 0 else valid_first
                q2 = jnp.concatenate([qs_ref[0, r0:r0 + BLOCK, :], qs_ref[1, r0:r0 + BLOCK, :]], axis=0)
                s = lax.dot_general(q2, kf_ref[k0:r0 + BLOCK, :], (((1,), (1,)), ((), ())),
                                    preferred_element_type=F32)
                s = jnp.where(valid, s, -jnp.inf)
                m = jnp.max(s, axis=-1, keepdims=True)
                if has_sink:
                    m = jnp.maximum(m, sk)
                p = jnp.exp(s - m)
                den = jnp.sum(p, axis=-1, keepdims=True)
                if has_sink:
                    den = den + jnp.exp(sk - m)
                pv = jnp.dot(p.astype(BF16), vf_ref[k0:r0 + BLOCK, :], preferred_element_type=F32)
                num = jnp.where(lo, pv[:BLOCK], pv[BLOCK:])
                den2 = jnp.where(lo, den[:BLOCK], den[BLOCK:])
                t0 = r + dil * BLOCK * jb
                if not multi:
                    o_ref[0, t0:t0 + BLOCK, :] = (num / den2).astype(o_ref.dtype)
                    continue
                m2 = jnp.where(lo, m[:BLOCK], m[BLOCK:])
                for ref, val in ((num_ref, num), (m_ref, m2), (den_ref, den2)):
                    if dil == 1:
                        ref[c, t0:t0 + BLOCK, :] = val
                    else:
                        ref[c, pl.ds(t0, BLOCK, stride=dil), :] = val

    if multi:
        step = 2 * BLOCK
        for t0 in range(0, seq, step):
            sl = slice(t0, t0 + step)
            ms = [m_ref[c, sl, :] for c in range(len(configs))]
            top = functools.reduce(jnp.maximum, ms)
            ws = [jnp.exp(mc - top) for mc in ms]
            numer = sum(w * num_ref[c, sl, :] for c, w in enumerate(ws))
            denom = sum(w * den_ref[c, sl, :] for c, w in enumerate(ws))
            o_ref[0, sl, :] = (numer / denom).astype(o_ref.dtype)


def _band_attention(qkv, *, batch, seq, q_col, k_col, v_col, kv_tile_of, configs, sink=None):
    n_tiles = 4
    qkv = qkv.reshape(batch, seq, qkv.shape[-1])
    tile = lambda col_of: pl.BlockSpec((1, seq, LANES), lambda b, t: (b, 0, col_of(t)))
    in_specs = [tile(lambda t: q_col // LANES + t),
                tile(lambda t: k_col // LANES + kv_tile_of(t)),
                tile(lambda t: v_col // LANES + kv_tile_of(t))]
    args = [qkv] * 3
    if sink is not None:
        sink_tiles = jnp.broadcast_to(sink.astype(F32).reshape(n_tiles, 2, 1), (n_tiles, 2, LANES))
        in_specs.append(pl.BlockSpec((1, 2, LANES), lambda b, t: (t, 0, 0)))
        args.append(sink_tiles)
    scratch = [pltpu.VMEM((2, seq, LANES), BF16), pltpu.VMEM((seq, LANES), BF16),
               pltpu.VMEM((seq, LANES), BF16)]
    if len(configs) > 1:
        scratch += [pltpu.VMEM((len(configs), seq, LANES), F32)] * 3
    out = pl.pallas_call(
        functools.partial(_band_kernel, configs=configs, seq=seq, has_sink=sink is not None),
        grid=(batch, n_tiles),
        in_specs=in_specs,
        out_specs=pl.BlockSpec((1, seq, LANES), lambda b, t: (b, 0, t)),
        out_shape=jax.ShapeDtypeStruct((batch, seq, n_tiles * LANES), BF16),
        scratch_shapes=scratch,
        compiler_params=_params("parallel", "arbitrary"),
        name="band_attention",
    )(*args)
    return out.reshape(batch * seq, n_tiles * LANES)


DSA_CHUNK = 2 * BLOCK


def _fold_rows(x, rows):
    parts = [x[r:r + rows] for r in range(0, x.shape[0], rows)]
    while len(parts) > 1:
        parts = [a + b for a, b in zip(parts[::2], parts[1::2])] + (parts[-1:] if len(parts) % 2 else [])
    return parts[0]


def _dsa_kernel(qc0_ref, qc1_ref, qi0_ref, qi1_ref, kc_ref, ki_ref, vc_ref, wi0_ref, wi1_ref, o_ref,
                qsc_ref, qsi_ref, wt_ref, key_ref, hi_ref, lo_ref, acc_ref, *, k_sel, n_tiles):
    j = pl.program_id(1)
    n_heads = C_HEADS
    ch = DSA_CHUNK
    n_chunks = (n_tiles * BLOCK) // ch + 1
    wide = n_heads * LANES
    tile_of = (j, n_tiles - 1 - j)
    n0 = (j * BLOCK + BLOCK + ch - 1) // ch
    lane = lax.broadcasted_iota(I32, (1, LANES), 1)
    lo_lanes = lane < HEAD_DIM

    for slot, (qc_ref, qi_ref, wi_ref) in enumerate(((qc0_ref, qi0_ref, wi0_ref),
                                                     (qc1_ref, qi1_ref, wi1_ref))):
        for h in range(n_heads):
            t = h // 2
            keep = lo_lanes if h % 2 == 0 else jnp.logical_not(lo_lanes)
            for src, dst in ((qc_ref, qsc_ref), (qi_ref, qsi_ref)):
                tile = src[:, t * LANES:(t + 1) * LANES].astype(F32)
                dst[slot, h * LANES:(h + 1) * LANES, :] = jnp.where(keep, tile, 0.0).astype(BF16)
        wt_ref[slot] = wi_ref[...].T

    def chunk_info(c):
        is1 = c >= n0
        slot = jnp.where(is1, 1, 0)
        k0 = pl.multiple_of((c - jnp.where(is1, n0, 0)) * ch, ch)
        q0 = jnp.where(is1, tile_of[1], tile_of[0]) * BLOCK
        return is1, slot, k0, q0

    info = [chunk_info(c) for c in range(n_chunks)]
    pick = lambda is1, a: jnp.where(is1, a[1], a[0])

    row = lax.broadcasted_iota(I32, (ch, LANES), 0)
    col = lax.broadcasted_iota(I32, (ch, LANES), 1)

    for c, (is1, slot, k0, q0) in enumerate(info):
        d = lax.dot_general(ki_ref[pl.ds(k0, ch), :], qsi_ref[slot],
                            (((1,), (1,)), ((), ())), preferred_element_type=F32)
        sc = jnp.zeros((ch, LANES), F32)
        for h in range(IDX_HEADS):
            sc = sc + jnp.maximum(d[:, h * LANES:(h + 1) * LANES], 0.0) * wt_ref[slot, h:h + 1, :]
        bits = pltpu.bitcast(sc, I32)
        key = bits ^ ((bits >> 31) & jnp.int32(0x7FFFFFFF))
        key = jnp.where(sc == 0.0, 0, key)
        key = jnp.where((k0 + row) <= (q0 + col), key, INT_MIN)
        key_ref[c] = key
        hi_ref[c] = (key >> 16).astype(I16)
        lo_ref[c] = ((key & 0xFFFF) - 32768).astype(I16)

    def count16(ref, cands, strict=False):
        acc = [jnp.zeros((16, LANES), I16), jnp.zeros((16, LANES), I16)]
        for c, (is1, _, _, _) in enumerate(info):
            cand = pick(is1, cands)
            blk = ref[c]
            hit = jnp.where(blk > cand if strict else blk >= cand, jnp.int16(1), jnp.int16(0))
            part = _fold_rows(hit, 16)
            zero = jnp.zeros_like(part)
            acc[0] = acc[0] + jnp.where(is1, zero, part)
            acc[1] = acc[1] + jnp.where(is1, part, zero)
        return [jnp.sum(a.astype(I32), axis=0, keepdims=True) for a in acc]

    def search16(ref, need):
        def step(it, tu):
            bit = lax.shift_left(jnp.int32(1), 15 - it)
            cu = [tu[0] | bit, tu[1] | bit]
            cnt = count16(ref, [(u - 32768).astype(I16) for u in cu])
            return tuple(jnp.where(cnt[s] >= need[s], cu[s], tu[s]) for s in range(2))
        z = jnp.zeros((1, LANES), I32)
        tu = lax.fori_loop(0, 16, step, (z, z))
        return [u - 32768 for u in tu]

    ksel = jnp.full((1, LANES), k_sel, I32)
    p1 = search16(hi_ref, [ksel, ksel])
    p1_16 = [p.astype(I16) for p in p1]
    above = count16(hi_ref, p1_16, strict=True)
    for c, (is1, _, _, _) in enumerate(info):
        lo_ref[c] = jnp.where(hi_ref[c] == pick(is1, p1_16), lo_ref[c], jnp.int16(I16_MIN))
    p2 = search16(lo_ref, [ksel - above[0], ksel - above[1]])
    thr = [jnp.maximum((p1[s] << 16) | ((p2[s] + 32768) & 0xFFFF), INT_MIN + 1) for s in range(2)]

    def count32(pred):
        acc = [jnp.zeros((8, LANES), I32), jnp.zeros((8, LANES), I32)]
        for c, (is1, _, _, _) in enumerate(info):
            hit = jnp.where(pred(key_ref[c], pick(is1, thr)), 1, 0)
            part = jnp.sum(hit.reshape(ch // 8, 8, LANES), axis=0)
            zero = jnp.zeros_like(part)
            acc[0] = acc[0] + jnp.where(is1, zero, part)
            acc[1] = acc[1] + jnp.where(is1, part, zero)
        return [jnp.sum(a, axis=0, keepdims=True) for a in acc]

    c_ge = count32(lambda k, t: k >= t)
    over = jnp.max(jnp.maximum(c_ge[0], c_ge[1])) > k_sel

    @pl.when(over)
    def _():
        c_gt = count32(lambda k, t: k > t)
        need = [(k_sel - c).astype(F32) for c in c_gt]
        tri = (lax.broadcasted_iota(I32, (ch, ch), 1)
               <= lax.broadcasted_iota(I32, (ch, ch), 0)).astype(BF16)
        carry = jnp.zeros((1, LANES), F32)
        for c, (is1, _, _, _) in enumerate(info):
            carry = jnp.where(n0 == c, 0.0, carry)
            key = key_ref[c]
            eq = jnp.where(key == pick(is1, thr), 1.0, 0.0)
            pre = jnp.dot(tri, eq.astype(BF16), preferred_element_type=F32) + carry
            drop = (eq > 0.0) & (pre > pick(is1, need))
            key_ref[c] = jnp.where(drop, INT_MIN, key)
            carry = carry + jnp.sum(eq, axis=0, keepdims=True)

    acc_ref[...] = jnp.zeros_like(acc_ref)
    m = [jnp.full((1, wide), -jnp.inf, F32)] * 2
    l = [jnp.zeros((1, wide), F32)] * 2
    for c, (is1, slot, k0, _) in enumerate(info):
        s = lax.dot_general(kc_ref[pl.ds(k0, ch), :], qsc_ref[slot],
                            (((1,), (1,)), ((), ())), preferred_element_type=F32)
        sel = key_ref[c] >= pick(is1, thr)
        s = jnp.where(jnp.tile(sel, (1, n_heads)), s, -jnp.inf)
        m_old = pick(is1, m)
        m_new = jnp.maximum(m_old, jnp.max(s, axis=0, keepdims=True))
        m_safe = jnp.where(m_new == -jnp.inf, 0.0, m_new)
        alpha = jnp.exp(m_old - m_safe)
        p = jnp.exp(s - m_safe)
        l_new = pick(is1, l) * alpha + jnp.sum(p, axis=0, keepdims=True)
        pv = lax.dot_general(vc_ref[pl.ds(k0, ch), :], p.astype(BF16),
                             (((0,), (0,)), ((), ())), preferred_element_type=F32)
        acc_ref[slot] = acc_ref[slot] * alpha + pv
        m = [jnp.where(is1, m[0], m_new), jnp.where(is1, m_new, m[1])]
        l = [jnp.where(is1, l[0], l_new), jnp.where(is1, l_new, l[1])]

    lo2 = lax.broadcasted_iota(I32, (BLOCK, LANES), 1) < HEAD_DIM
    for slot in range(2):
        o_t = acc_ref[slot] / l[slot]
        for t in range(n_heads // 2):
            ev = o_t[:, (2 * t) * LANES:(2 * t + 1) * LANES].T
            od = o_t[:, (2 * t + 1) * LANES:(2 * t + 2) * LANES].T
            o_ref[0, tile_of[slot], :, t * LANES:(t + 1) * LANES] = (
                jnp.where(lo2, ev, od).astype(o_ref.dtype))


def _dsa(c_pack, wi, batch, seq):
    n = batch * seq
    nq = seq // BLOCK
    assert nq % 2 == 0 and (nq * BLOCK) % DSA_CHUNK == 0
    n_chunks = (nq * BLOCK) // DSA_CHUNK + 1
    k_sel = min(TOPK_MAX, seq // 4)
    qw = C_HEADS * HEAD_DIM
    first = lambda b, j: b * nq + j
    second = lambda b, j: b * nq + nq - 1 - j
    qrow = lambda blk, which: pl.BlockSpec((BLOCK, qw), lambda b, j: (which(b, j), blk))
    wrow = lambda which: pl.BlockSpec((BLOCK, LANES), lambda b, j: (which(b, j), 0))
    full = lambda col: pl.BlockSpec((seq, LANES), lambda b, j: (b, col // LANES))
    out = pl.pallas_call(
        functools.partial(_dsa_kernel, k_sel=k_sel, n_tiles=nq),
        grid=(batch, nq // 2),
        in_specs=[qrow(0, first), qrow(0, second), qrow(1, first), qrow(1, second),
                  full(C_KC), full(C_KI), full(C_VC), wrow(first), wrow(second)],
        out_specs=pl.BlockSpec((1, nq, BLOCK, qw), lambda b, j: (b, 0, 0, 0)),
        out_shape=jax.ShapeDtypeStruct((batch, nq, BLOCK, qw), BF16),
        scratch_shapes=[
            pltpu.VMEM((2, C_HEADS * LANES, LANES), BF16),
            pltpu.VMEM((2, IDX_HEADS * LANES, LANES), BF16),
            pltpu.VMEM((2, LANES, LANES), F32),
            pltpu.VMEM((n_chunks, DSA_CHUNK, LANES), I32),
            pltpu.VMEM((n_chunks, DSA_CHUNK, LANES), I16),
            pltpu.VMEM((n_chunks, DSA_CHUNK, LANES), I16),
            pltpu.VMEM((2, LANES, C_HEADS * LANES), F32),
        ],
        compiler_params=_params("parallel", "arbitrary"),
        name="dsa",
    )(c_pack, c_pack, c_pack, c_pack, c_pack, c_pack, c_pack, wi, wi)
    return out.reshape(n, qw)


def _merge_kernel(x_ref, oa_ref, ob_ref, oc_ref,
                  wg_ref, wa_ref, wb_ref, wc_ref, wo_ref, g_ref, b_ref, o_ref, *, alpha):
    x = x_ref[...]
    xb = x.astype(BF16)
    d = x.shape[1]

    def branch(o_ref_, w_ref, k):
        gate = jnp.dot(xb, wg_ref[:, k * d:(k + 1) * d], preferred_element_type=F32)
        return jax.nn.sigmoid(gate) * jnp.dot(o_ref_[...], w_ref[...], preferred_element_type=F32)

    merged = branch(oa_ref, wa_ref, 0) + branch(ob_ref, wb_ref, 1) + branch(oc_ref, wc_ref, 2)
    out = jnp.dot(merged.astype(BF16), wo_ref[...], preferred_element_type=F32)
    o_ref[...] = _layer_norm(alpha * x + out, g_ref[...], b_ref[...])


def _merge(x, oa, ob, oc, w_gates, w_a, w_b, w_c, w_o, g, b, alpha):
    n, d = x.shape
    tm = 512
    bw = oa.shape[1]
    row = lambda w: pl.BlockSpec((tm, w), lambda i: (i, 0))
    const = lambda a: pl.BlockSpec(a.shape, lambda i: (0, 0))
    return pl.pallas_call(
        functools.partial(_merge_kernel, alpha=alpha),
        grid=(n // tm,),
        in_specs=[row(d)] + [row(bw)] * 3
        + [const(w_gates), const(w_a), const(w_b), const(w_c), const(w_o), const(g), const(b)],
        out_specs=row(d),
        out_shape=jax.ShapeDtypeStruct((n, d), F32),
        compiler_params=_params("parallel"),
        name="merge_ln",
    )(x, oa, ob, oc, w_gates, w_a, w_b, w_c, w_o, g, b)


def _mixer(x, cos, sin, w_in, sink_b, w_br_a, w_br_b, w_br_c, w_out, g, b, alpha, batch, seq):
    n = batch * seq
    w_packed, w_gates = _pack_in_weights(w_in)
    a_pack, b_pack, c_pack, wi = _inproj(x, w_packed, cos, sin)

    oa = _band_attention(a_pack, batch=batch, seq=seq, q_col=0, k_col=A_W, v_col=2 * A_W,
                         kv_tile_of=lambda t: t,
                         configs=tuple((dil, window // dil) for window, dil in A_CONFIGS))
    ob = _band_attention(b_pack, batch=batch, seq=seq, q_col=0, k_col=512, v_col=768,
                         kv_tile_of=lambda t: t // 2, configs=((1, B_WINDOW - 1),), sink=sink_b)
    oc = _dsa(c_pack, wi, batch, seq)

    return _merge(x, oa, ob, oc, w_gates, w_br_a.astype(BF16), w_br_b.astype(BF16),
                  w_br_c.astype(BF16), w_out.astype(BF16), g, b, alpha)


def kernel(x, positions, w_in, sink_b, w_br_a, w_br_b, w_br_c, w_out,
           ffn1_in, ffn1_out, ffn2_in, ffn2_out, ln_g, ln_b):
    batch, seq, d = x.shape
    depth = w_in.shape[0]
    alpha = float((2 * depth) ** 0.25)
    cos, sin = _rope_tables(positions)
    h = x.reshape(batch * seq, d)
    for l in range(depth):
        gb = lambda k: (ln_g[l, k][None, :], ln_b[l, k][None, :])
        h = _ffn(h, ffn1_in[l].astype(BF16), ffn1_out[l].astype(BF16), *gb(0), alpha)
        h = _mixer(h, cos, sin, w_in[l], sink_b[l], w_br_a[l], w_br_b[l], w_br_c[l], w_out[l],
                   *gb(1), alpha, batch, seq)
        h = _ffn(h, ffn2_in[l].astype(BF16), ffn2_out[l].astype(BF16), *gb(2), alpha)
    return h.reshape(batch, seq, d)
```

```python
import functools

import jax
import jax.numpy as jnp
import numpy as np
from jax import lax
from jax.experimental import pallas as pl
from jax.experimental.pallas import tpu as pltpu

HEAD_DIM = 64
ROPE_THETA = 10000.0
A_HEADS = 8
A_CONFIGS = ((128, 1), (512, 4), (2048, 16))
B_HEADS = 8
B_KV_HEADS = 2
B_WINDOW = 128
C_HEADS = 8
IDX_HEADS = 8
TOPK_MAX = 256
BLOCK = 128
LN_EPS = 1e-5
LANES = 128
VMEM_LIMIT_BYTES = 56 * 1024 * 1024
INT_MIN = -(2 ** 31)
I16_MIN = -(2 ** 15)
LOG2E = 1.4426950408889634

F32 = jnp.float32
BF16 = jnp.bfloat16
I16 = jnp.int16
I32 = jnp.int32


def _params(*sem):
    return pltpu.CompilerParams(dimension_semantics=sem, vmem_limit_bytes=VMEM_LIMIT_BYTES)


def _layer_norm(y, g, b):
    mu = jnp.mean(y, axis=-1, keepdims=True)
    d = y - mu
    var = jnp.mean(d * d, axis=-1, keepdims=True)
    return d * lax.rsqrt(var + LN_EPS) * g + b


def _rope_table_kernel(pos_ref, inv_ref, sign_ref, cos_ref, sin_ref):
    ang = pos_ref[...].astype(F32) * inv_ref[...]
    cos_ref[...] = jnp.cos(ang)
    sin_ref[...] = jnp.sin(ang) * sign_ref[...]


def _rope_tables(positions):
    n = positions.size
    tm = 1024
    inv = ROPE_THETA ** (-jnp.arange(0, HEAD_DIM, 2, dtype=F32) / HEAD_DIM)
    inv = jnp.tile(inv, LANES // inv.shape[0])[None, :]
    half = (np.arange(LANES) % HEAD_DIM) < HEAD_DIM // 2
    sign = jnp.asarray(np.where(half, -1.0, 1.0), F32)[None, :]
    row = pl.BlockSpec((tm, LANES), lambda i: (i, 0))
    const = pl.BlockSpec((1, LANES), lambda i: (0, 0))
    return pl.pallas_call(
        _rope_table_kernel,
        grid=(n // tm,),
        in_specs=[pl.BlockSpec((tm, 1), lambda i: (i, 0)), const, const],
        out_specs=[row, row],
        out_shape=[jax.ShapeDtypeStruct((n, LANES), F32)] * 2,
        compiler_params=_params("parallel"),
        name="rope_tables",
    )(positions.reshape(n, 1), inv, sign)


FFN_SUB = 256


def _ffn_kernel(x_ref, wg_ref, wu_ref, wo_ref, g_ref, b_ref, o_ref, *, alpha):
    for r0 in range(0, x_ref.shape[0], FFN_SUB):
        x = x_ref[r0:r0 + FFN_SUB, :]
        xb = x.astype(BF16)
        gate = jnp.dot(xb, wg_ref[...], preferred_element_type=F32)
        up = jnp.dot(xb, wu_ref[...], preferred_element_type=F32)
        h = (gate * jax.nn.sigmoid(gate) * up).astype(BF16)
        y = alpha * x + 0.5 * jnp.dot(h, wo_ref[...], preferred_element_type=F32)
        o_ref[r0:r0 + FFN_SUB, :] = _layer_norm(y, g_ref[...], b_ref[...])


def _resident(shape, index_map):
    return pl.BlockSpec(shape, index_map, pipeline_mode=pl.Buffered(1))


def _ffn(x, w_in, w_out, g, b, alpha):
    n, d = x.shape
    d_ff = w_out.shape[0]
    tm = 512
    return pl.pallas_call(
        functools.partial(_ffn_kernel, alpha=alpha),
        grid=(n // tm,),
        in_specs=[
            pl.BlockSpec((tm, d), lambda i: (i, 0)),
            _resident((d, d_ff), lambda i: (0, 0)),
            _resident((d, d_ff), lambda i: (0, 1)),
            _resident((d_ff, d), lambda i: (0, 0)),
            _resident((1, d), lambda i: (0, 0)),
            _resident((1, d), lambda i: (0, 0)),
        ],
        out_specs=pl.BlockSpec((tm, d), lambda i: (i, 0)),
        out_shape=jax.ShapeDtypeStruct((n, d), F32),
        compiler_params=_params("parallel"),
        name="ffn_ln",
    )(x, w_in, w_in, w_out, g, b)


A_W = A_HEADS * HEAD_DIM
PACK_A = 3 * A_W
PACK_B = B_HEADS * HEAD_DIM + 4 * LANES
PACK_C = 2 * C_HEADS * HEAD_DIM + 3 * LANES
C_KC, C_KI, C_VC = 1024, 1152, 1280


def _pack_in_weights(w):
    sizes = (A_W, A_W, A_W, 512, 128, 128, 512, 64, 64, 512, 64, 8, 1024, 1024, 1024)
    offs = np.concatenate([[0], np.cumsum(sizes)])
    (qa, ka, va, qb, kb, vb, qc, kc, vc, qi, ki, wi, ga, gb, gc) = [
        w[:, offs[i]:offs[i + 1]] for i in range(len(sizes))]

    def dup_heads(t):
        g = t.shape[1] // HEAD_DIM
        t = t.reshape(t.shape[0], g, 1, HEAD_DIM)
        return jnp.broadcast_to(t, (t.shape[0], g, 2, HEAD_DIM)).reshape(t.shape[0], g * LANES)

    wi_pad = jnp.pad(wi, ((0, 0), (0, LANES - wi.shape[1])))
    packed = jnp.concatenate(
        [qa, ka, va, qb, dup_heads(kb), dup_heads(vb),
         qc, qi, dup_heads(kc), dup_heads(ki), dup_heads(vc), wi_pad], axis=1)
    gates = jnp.concatenate([ga, gb, gc], axis=1)
    return packed.astype(BF16), gates.astype(BF16)


def _rope(h, cos, sin):
    w = h.shape[1]
    reps = w // LANES
    c = jnp.tile(cos, (1, reps)) if reps > 1 else cos
    s = jnp.tile(sin, (1, reps)) if reps > 1 else sin
    lane = lax.broadcasted_iota(jnp.int32, h.shape, 1)
    first = (lane % HEAD_DIM) < HEAD_DIM // 2
    partner = jnp.where(first, pltpu.roll(h, w - HEAD_DIM // 2, 1), pltpu.roll(h, HEAD_DIM // 2, 1))
    return h * c + partner * s


def _inproj_kernel(x_ref, w_ref, cos_ref, sin_ref, a_ref, b_ref, c_ref, wi_ref):
    xb = x_ref[...].astype(BF16)
    cos = cos_ref[...]
    sin = sin_ref[...]
    scale = HEAD_DIM ** -0.5

    def proj(c0, width):
        return jnp.dot(xb, w_ref[:, c0:c0 + width], preferred_element_type=F32)

    a_ref[:, 0:512] = _rope(proj(0, 512), cos, sin) * scale
    a_ref[:, 512:1024] = _rope(proj(512, 512), cos, sin)
    a_ref[:, 1024:1536] = proj(1024, 512)
    o = PACK_A
    b_ref[:, 0:512] = (_rope(proj(o, 512), cos, sin) * scale).astype(BF16)
    b_ref[:, 512:768] = _rope(proj(o + 512, 256), cos, sin).astype(BF16)
    b_ref[:, 768:1024] = proj(o + 768, 256).astype(BF16)
    o = PACK_A + PACK_B
    c_ref[:, 0:512] = (_rope(proj(o, 512), cos, sin) * (scale * LOG2E)).astype(BF16)
    c_ref[:, 512:1024] = _rope(proj(o + 512, 512), cos, sin).astype(BF16)
    c_ref[:, 1024:1280] = _rope(proj(o + 1024, 256), cos, sin).astype(BF16)
    lane = lax.broadcasted_iota(I32, (xb.shape[0], LANES), 1)
    c_ref[:, 1280:1408] = jnp.where(lane < HEAD_DIM, proj(o + 1280, 128), 1.0).astype(BF16)
    wi_ref[...] = proj(o + 1408, 128)


def _inproj(x, w_packed, cos, sin):
    n, d = x.shape
    tm = 512
    wtot = w_packed.shape[1]
    row = lambda w: pl.BlockSpec((tm, w), lambda i: (i, 0))
    return pl.pallas_call(
        _inproj_kernel,
        grid=(n // tm,),
        in_specs=[row(d), pl.BlockSpec((d, wtot), lambda i: (0, 0)), row(LANES), row(LANES)],
        out_specs=[row(PACK_A), row(PACK_B), row(PACK_C), row(LANES)],
        out_shape=[
            jax.ShapeDtypeStruct((n, PACK_A), F32),
            jax.ShapeDtypeStruct((n, PACK_B), BF16),
            jax.ShapeDtypeStruct((n, PACK_C), BF16),
            jax.ShapeDtypeStruct((n, LANES), F32),
        ],
        compiler_params=_params("parallel"),
        name="inproj_rope",
    )(x, w_packed, cos, sin)


def _band_kernel(*refs, configs, seq, has_sink):
    q_ref, k_ref, v_ref = refs[:3]
    rest = refs[3:]
    if has_sink:
        sink_ref, rest = rest[0], rest[1:]
    o_ref, qs_ref, kf_ref, vf_ref = rest[:4]
    multi = len(configs) > 1
    if multi:
        num_ref, m_ref, den_ref = rest[4:7]

    lane = lax.broadcasted_iota(I32, (1, LANES), 1)
    lo = lane < HEAD_DIM
    if has_sink:
        sk = jnp.max(jnp.concatenate([jnp.broadcast_to(sink_ref[0, 0:1, :], (BLOCK, LANES)),
                                      jnp.broadcast_to(sink_ref[0, 1:2, :], (BLOCK, LANES))], axis=0),
                     axis=-1, keepdims=True)

    def valid_mask(nk, max_dist):
        qi = lax.broadcasted_iota(I32, (BLOCK, nk), 0)
        kj = lax.broadcasted_iota(I32, (BLOCK, nk), 1)
        dist = qi + (nk - BLOCK) - kj
        ok = (dist >= 0) & (dist <= max_dist)
        return jnp.concatenate([ok, ok], axis=0)

    def rows(ref, start, size, stride):
        if stride == 1:
            return ref[0, start:start + size, :]
        return ref[0, pl.ds(start, size, stride=stride), :]

    for c, (dil, max_dist) in enumerate(configs):
        fl = seq // dil
        for r in range(dil):
            dst = slice(r * fl, (r + 1) * fl)
            x = rows(q_ref, r, fl, dil).astype(F32)
            qs_ref[0, dst, :] = jnp.where(lo, x, 0.0).astype(BF16)
            qs_ref[1, dst, :] = jnp.where(lo, 0.0, x).astype(BF16)
            kf_ref[dst, :] = rows(k_ref, r, fl, dil).astype(BF16)
            vf_ref[dst, :] = rows(v_ref, r, fl, dil).astype(BF16)
        valid_full = valid_mask(2 * BLOCK, max_dist)
        valid_first = valid_mask(BLOCK, max_dist)
        for r in range(dil):
            for jb in range(fl // BLOCK):
                r0 = r * fl + jb * BLOCK
                k0 = r0 - BLOCK if jb > 0 else r0
                valid = valid_full if jb > 0 else valid_first
                q2 = jnp.concatenate([qs_ref[0, r0:r0 + BLOCK, :], qs_ref[1, r0:r0 + BLOCK, :]], axis=0)
                s = lax.dot_general(q2, kf_ref[k0:r0 + BLOCK, :], (((1,), (1,)), ((), ())),
                                    preferred_element_type=F32)
                s = jnp.where(valid, s, -jnp.inf)
                m = jnp.max(s, axis=-1, keepdims=True)
                if has_sink:
                    m = jnp.maximum(m, sk)
                p = jnp.exp(s - m)
                den = jnp.sum(p, axis=-1, keepdims=True)
                if has_sink:
                    den = den + jnp.exp(sk - m)
                pv = jnp.dot(p.astype(BF16), vf_ref[k0:r0 + BLOCK, :], preferred_element_type=F32)
                num = jnp.where(lo, pv[:BLOCK], pv[BLOCK:])
                den2 = jnp.where(lo, den[:BLOCK], den[BLOCK:])
                t0 = r + dil * BLOCK * jb
                if not multi:
                    o_ref[0, t0:t0 + BLOCK, :] = (num / den2).astype(o_ref.dtype)
                    continue
                m2 = jnp.where(lo, m[:BLOCK], m[BLOCK:])
                for ref, val in ((num_ref, num), (m_ref, m2), (den_ref, den2)):
                    if dil == 1:
                        ref[c, t0:t0 + BLOCK, :] = val
                    else:
                        ref[c, pl.ds(t0, BLOCK, stride=dil), :] = val

    if multi:
        step = 2 * BLOCK
        for t0 in range(0, seq, step):
            sl = slice(t0, t0 + step)
            ms = [m_ref[c, sl, :] for c in range(len(configs))]
            top = functools.reduce(jnp.maximum, ms)
            ws = [jnp.exp(mc - top) for mc in ms]
            numer = sum(w * num_ref[c, sl, :] for c, w in enumerate(ws))
            denom = sum(w * den_ref[c, sl, :] for c, w in enumerate(ws))
            o_ref[0, sl, :] = (numer / denom).astype(o_ref.dtype)


def _band_attention(qkv, *, batch, seq, q_col, k_col, v_col, kv_tile_of, configs, sink=None):
    n_tiles = 4
    qkv = qkv.reshape(batch, seq, qkv.shape[-1])
    tile = lambda col_of: pl.BlockSpec((1, seq, LANES), lambda b, t: (b, 0, col_of(t)))
    in_specs = [tile(lambda t: q_col // LANES + t),
                tile(lambda t: k_col // LANES + kv_tile_of(t)),
                tile(lambda t: v_col // LANES + kv_tile_of(t))]
    args = [qkv] * 3
    if sink is not None:
        sink_tiles = jnp.broadcast_to(sink.astype(F32).reshape(n_tiles, 2, 1), (n_tiles, 2, LANES))
        in_specs.append(pl.BlockSpec((1, 2, LANES), lambda b, t: (t, 0, 0)))
        args.append(sink_tiles)
    scratch = [pltpu.VMEM((2, seq, LANES), BF16), pltpu.VMEM((seq, LANES), BF16),
               pltpu.VMEM((seq, LANES), BF16)]
    if len(configs) > 1:
        scratch += [pltpu.VMEM((len(configs), seq, LANES), F32)] * 3
    out = pl.pallas_call(
        functools.partial(_band_kernel, configs=configs, seq=seq, has_sink=sink is not None),
        grid=(batch, n_tiles),
        in_specs=in_specs,
        out_specs=pl.BlockSpec((1, seq, LANES), lambda b, t: (b, 0, t)),
        out_shape=jax.ShapeDtypeStruct((batch, seq, n_tiles * LANES), BF16),
        scratch_shapes=scratch,
        compiler_params=_params("parallel", "arbitrary"),
        name="band_attention",
    )(*args)
    return out.reshape(batch * seq, n_tiles * LANES)


DSA_CHUNK = 2 * BLOCK
BOUND_SLACK = 1.03
DEN_FLOOR = 2.0 ** -60


def _fold_rows(x, rows):
    parts = [x[r:r + rows] for r in range(0, x.shape[0], rows)]
    while len(parts) > 1:
        parts = [a + b for a, b in zip(parts[::2], parts[1::2])] + (parts[-1:] if len(parts) % 2 else [])
    return parts[0]


def _dsa_kernel(qc0_ref, qc1_ref, qi0_ref, qi1_ref, kc_ref, ki_ref, vc_ref, wi0_ref, wi1_ref, o_ref,
                qsc_ref, qsi_ref, wt_ref, key_ref, hi_ref, lo_ref, acc_ref, *, k_sel, n_tiles):
    j = pl.program_id(1)
    n_heads = C_HEADS
    ch = DSA_CHUNK
    n_chunks = (n_tiles * BLOCK) // ch + 1
    wide = n_heads * LANES
    tile_of = (j, n_tiles - 1 - j)
    n0 = (j * BLOCK + BLOCK + ch - 1) // ch
    lane = lax.broadcasted_iota(I32, (1, LANES), 1)
    lo_lanes = lane < HEAD_DIM

    for slot, (qc_ref, qi_ref, wi_ref) in enumerate(((qc0_ref, qi0_ref, wi0_ref),
                                                     (qc1_ref, qi1_ref, wi1_ref))):
        for h in range(n_heads):
            t = h // 2
            keep = lo_lanes if h % 2 == 0 else jnp.logical_not(lo_lanes)
            for src, dst in ((qc_ref, qsc_ref), (qi_ref, qsi_ref)):
                tile = src[:, t * LANES:(t + 1) * LANES].astype(F32)
                dst[slot, h * LANES:(h + 1) * LANES, :] = jnp.where(keep, tile, 0.0).astype(BF16)
        wt_ref[slot] = wi_ref[...].T

    def chunk_info(c):
        is1 = c >= n0
        slot = jnp.where(is1, 1, 0)
        k0 = pl.multiple_of((c - jnp.where(is1, n0, 0)) * ch, ch)
        q0 = jnp.where(is1, tile_of[1], tile_of[0]) * BLOCK
        return is1, slot, k0, q0

    info = [chunk_info(c) for c in range(n_chunks)]
    pick = lambda is1, a: jnp.where(is1, a[1], a[0])

    row_minus_col = (lax.broadcasted_iota(I32, (ch, LANES), 0)
                     - lax.broadcasted_iota(I32, (ch, LANES), 1))

    for c, (is1, slot, k0, q0) in enumerate(info):
        d = lax.dot_general(ki_ref[pl.ds(k0, ch), :], qsi_ref[slot],
                            (((1,), (1,)), ((), ())), preferred_element_type=F32)
        sc = jnp.zeros((ch, LANES), F32)
        for h in range(IDX_HEADS):
            sc = sc + jnp.maximum(d[:, h * LANES:(h + 1) * LANES], 0.0) * wt_ref[slot, h:h + 1, :]
        bits = pltpu.bitcast(sc, I32)
        key = bits ^ ((bits >> 31) & jnp.int32(0x7FFFFFFF))
        key = jnp.where(sc == 0.0, 0, key)
        key = jnp.where(row_minus_col <= q0 - k0, key, INT_MIN)
        key_ref[c] = key
        hi_ref[c] = (key >> 16).astype(I16)
        lo_ref[c] = ((key & 0xFFFF) - 32768).astype(I16)

    def count16(ref, cands, strict=False):
        acc = [jnp.zeros((16, LANES), I16), jnp.zeros((16, LANES), I16)]
        for c, (is1, _, _, _) in enumerate(info):
            cand = pick(is1, cands)
            blk = ref[c]
            hit = jnp.where(blk > cand if strict else blk >= cand, jnp.int16(1), jnp.int16(0))
            part = _fold_rows(hit, 16)
            zero = jnp.zeros_like(part)
            acc[0] = acc[0] + jnp.where(is1, zero, part)
            acc[1] = acc[1] + jnp.where(is1, part, zero)
        return [jnp.sum(a.astype(I32), axis=0, keepdims=True) for a in acc]

    def search16(ref, need):
        def step(it, tu):
            bit = lax.shift_left(jnp.int32(1), 15 - it)
            cu = [tu[0] | bit, tu[1] | bit]
            cnt = count16(ref, [(u - 32768).astype(I16) for u in cu])
            return tuple(jnp.where(cnt[s] >= need[s], cu[s], tu[s]) for s in range(2))
        z = jnp.zeros((1, LANES), I32)
        tu = lax.fori_loop(0, 16, step, (z, z))
        return [u - 32768 for u in tu]

    ksel = jnp.full((1, LANES), k_sel, I32)
    p1 = search16(hi_ref, [ksel, ksel])
    p1_16 = [p.astype(I16) for p in p1]
    above = count16(hi_ref, p1_16, strict=True)
    for c, (is1, _, _, _) in enumerate(info):
        lo_ref[c] = jnp.where(hi_ref[c] == pick(is1, p1_16), lo_ref[c], jnp.int16(I16_MIN))
    p2 = search16(lo_ref, [ksel - above[0], ksel - above[1]])
    thr = [jnp.maximum((p1[s] << 16) | ((p2[s] + 32768) & 0xFFFF), INT_MIN + 1) for s in range(2)]

    def count32(pred):
        acc = [jnp.zeros((8, LANES), I32), jnp.zeros((8, LANES), I32)]
        for c, (is1, _, _, _) in enumerate(info):
            hit = jnp.where(pred(key_ref[c], pick(is1, thr)), 1, 0)
            part = jnp.sum(hit.reshape(ch // 8, 8, LANES), axis=0)
            zero = jnp.zeros_like(part)
            acc[0] = acc[0] + jnp.where(is1, zero, part)
            acc[1] = acc[1] + jnp.where(is1, part, zero)
        return [jnp.sum(a, axis=0, keepdims=True) for a in acc]

    c_ge = count32(lambda k, t: k >= t)
    over = jnp.max(jnp.maximum(c_ge[0], c_ge[1])) > k_sel

    @pl.when(over)
    def _():
        c_gt = count32(lambda k, t: k > t)
        need = [(k_sel - c).astype(F32) for c in c_gt]
        tri = (lax.broadcasted_iota(I32, (ch, ch), 1)
               <= lax.broadcasted_iota(I32, (ch, ch), 0)).astype(BF16)
        carry = jnp.zeros((1, LANES), F32)
        for c, (is1, _, _, _) in enumerate(info):
            carry = jnp.where(n0 == c, 0.0, carry)
            key = key_ref[c]
            eq = jnp.where(key == pick(is1, thr), 1.0, 0.0)
            pre = jnp.dot(tri, eq.astype(BF16), preferred_element_type=F32) + carry
            drop = (eq > 0.0) & (pre > pick(is1, need))
            key_ref[c] = jnp.where(drop, INT_MIN, key)
            carry = carry + jnp.sum(eq, axis=0, keepdims=True)

    def logits(c):
        _, slot, k0, _ = info[c]
        return lax.dot_general(kc_ref[pl.ds(k0, ch), :], qsc_ref[slot],
                               (((1,), (1,)), ((), ())), preferred_element_type=F32)

    def selected(c):
        return jnp.tile(key_ref[c] >= pick(info[c][0], thr), (1, n_heads))

    def weighted_values(c, p):
        return lax.dot_general(vc_ref[pl.ds(info[c][2], ch), :], p.astype(BF16),
                               (((0,), (0,)), ((), ())), preferred_element_type=F32)

    ksq = kc_ref[...].astype(F32)
    ksq = 0.5 * jnp.max(jnp.sum(ksq * ksq, axis=1, keepdims=True), axis=0, keepdims=True)
    ones8 = jnp.ones((8, LANES), BF16)
    bound = []
    for slot in range(2):
        q2 = qsc_ref[slot].astype(F32)
        qsq = lax.dot_general(ones8, (q2 * q2).astype(BF16), (((1,), (1,)), ((), ())),
                              preferred_element_type=F32)[0:1, :]
        bound.append(jnp.sqrt(qsq * ksq) * BOUND_SLACK)
    acc_ref[...] = jnp.zeros_like(acc_ref)
    for c, (is1, slot, _, _) in enumerate(info):
        p = jnp.where(selected(c), jnp.exp2(logits(c) - pick(is1, bound)), 0.0)
        acc_ref[slot] += weighted_values(c, p)
    den_min = jnp.min(jnp.minimum(acc_ref[0, HEAD_DIM:HEAD_DIM + 1, :], acc_ref[1, HEAD_DIM:HEAD_DIM + 1, :]))

    @pl.when(jnp.logical_not(den_min > DEN_FLOOR))
    def _():
        acc_ref[...] = jnp.zeros_like(acc_ref)
        m = [jnp.full((1, wide), -jnp.inf, F32)] * 2
        for c, (is1, slot, _, _) in enumerate(info):
            s = jnp.where(selected(c), logits(c), -jnp.inf)
            m_old = pick(is1, m)
            m_new = jnp.maximum(m_old, jnp.max(s, axis=0, keepdims=True))
            m_safe = jnp.where(m_new == -jnp.inf, 0.0, m_new)
            acc_ref[slot] = acc_ref[slot] * jnp.exp2(m_old - m_safe) + weighted_values(c, jnp.exp2(s - m_safe))
            m = [jnp.where(is1, m[0], m_new), jnp.where(is1, m_new, m[1])]

    lo2 = lax.broadcasted_iota(I32, (BLOCK, LANES), 1) < HEAD_DIM
    for slot in range(2):
        acc = acc_ref[slot]
        o_t = acc / acc[HEAD_DIM:HEAD_DIM + 1, :]
        for t in range(n_heads // 2):
            ev = o_t[:, (2 * t) * LANES:(2 * t + 1) * LANES].T
            od = o_t[:, (2 * t + 1) * LANES:(2 * t + 2) * LANES].T
            o_ref[0, tile_of[slot], :, t * LANES:(t + 1) * LANES] = (
                jnp.where(lo2, ev, pltpu.roll(od, HEAD_DIM, 1)).astype(o_ref.dtype))


def _dsa(c_pack, wi, batch, seq):
    n = batch * seq
    nq = seq // BLOCK
    assert nq % 2 == 0 and (nq * BLOCK) % DSA_CHUNK == 0
    n_chunks = (nq * BLOCK) // DSA_CHUNK + 1
    k_sel = min(TOPK_MAX, seq // 4)
    qw = C_HEADS * HEAD_DIM
    first = lambda b, j: b * nq + j
    second = lambda b, j: b * nq + nq - 1 - j
    qrow = lambda blk, which: pl.BlockSpec((BLOCK, qw), lambda b, j: (which(b, j), blk))
    wrow = lambda which: pl.BlockSpec((BLOCK, LANES), lambda b, j: (which(b, j), 0))
    full = lambda col: pl.BlockSpec((seq, LANES), lambda b, j: (b, col // LANES))
    out = pl.pallas_call(
        functools.partial(_dsa_kernel, k_sel=k_sel, n_tiles=nq),
        grid=(batch, nq // 2),
        in_specs=[qrow(0, first), qrow(0, second), qrow(1, first), qrow(1, second),
                  full(C_KC), full(C_KI), full(C_VC), wrow(first), wrow(second)],
        out_specs=pl.BlockSpec((1, nq, BLOCK, qw), lambda b, j: (b, 0, 0, 0)),
        out_shape=jax.ShapeDtypeStruct((batch, nq, BLOCK, qw), BF16),
        scratch_shapes=[
            pltpu.VMEM((2, C_HEADS * LANES, LANES), BF16),
            pltpu.VMEM((2, IDX_HEADS * LANES, LANES), BF16),
            pltpu.VMEM((2, LANES, LANES), F32),
            pltpu.VMEM((n_chunks, DSA_CHUNK, LANES), I32),
            pltpu.VMEM((n_chunks, DSA_CHUNK, LANES), I16),
            pltpu.VMEM((n_chunks, DSA_CHUNK, LANES), I16),
            pltpu.VMEM((2, LANES, C_HEADS * LANES), F32),
        ],
        compiler_params=_params("parallel", "arbitrary"),
        name="dsa",
    )(c_pack, c_pack, c_pack, c_pack, c_pack, c_pack, c_pack, wi, wi)
    return out.reshape(n, qw)


def _merge_kernel(x_ref, oa_ref, ob_ref, oc_ref,
                  wg_ref, wa_ref, wb_ref, wc_ref, wo_ref, g_ref, b_ref, o_ref, *, alpha):
    x = x_ref[...]
    xb = x.astype(BF16)
    d = x.shape[1]

    def branch(o_ref_, w_ref, k):
        gate = jnp.dot(xb, wg_ref[:, k * d:(k + 1) * d], preferred_element_type=F32)
        return jax.nn.sigmoid(gate) * jnp.dot(o_ref_[...], w_ref[...], preferred_element_type=F32)

    merged = branch(oa_ref, wa_ref, 0) + branch(ob_ref, wb_ref, 1) + branch(oc_ref, wc_ref, 2)
    out = jnp.dot(merged.astype(BF16), wo_ref[...], preferred_element_type=F32)
    o_ref[...] = _layer_norm(alpha * x + out, g_ref[...], b_ref[...])


def _merge(x, oa, ob, oc, w_gates, w_a, w_b, w_c, w_o, g, b, alpha):
    n, d = x.shape
    tm = 512
    bw = oa.shape[1]
    row = lambda w: pl.BlockSpec((tm, w), lambda i: (i, 0))
    const = lambda a: pl.BlockSpec(a.shape, lambda i: (0, 0))
    return pl.pallas_call(
        functools.partial(_merge_kernel, alpha=alpha),
        grid=(n // tm,),
        in_specs=[row(d)] + [row(bw)] * 3
        + [const(w_gates), const(w_a), const(w_b), const(w_c), const(w_o), const(g), const(b)],
        out_specs=row(d),
        out_shape=jax.ShapeDtypeStruct((n, d), F32),
        compiler_params=_params("parallel"),
        name="merge_ln",
    )(x, oa, ob, oc, w_gates, w_a, w_b, w_c, w_o, g, b)


def _mixer(x, cos, sin, w_in, sink_b, w_br_a, w_br_b, w_br_c, w_out, g, b, alpha, batch, seq):
    n = batch * seq
    w_packed, w_gates = _pack_in_weights(w_in)
    a_pack, b_pack, c_pack, wi = _inproj(x, w_packed, cos, sin)

    oa = _band_attention(a_pack, batch=batch, seq=seq, q_col=0, k_col=A_W, v_col=2 * A_W,
                         kv_tile_of=lambda t: t,
                         configs=tuple((dil, window // dil) for window, dil in A_CONFIGS))
    ob = _band_attention(b_pack, batch=batch, seq=seq, q_col=0, k_col=512, v_col=768,
                         kv_tile_of=lambda t: t // 2, configs=((1, B_WINDOW - 1),), sink=sink_b)
    oc = _dsa(c_pack, wi, batch, seq)

    return _merge(x, oa, ob, oc, w_gates, w_br_a.astype(BF16), w_br_b.astype(BF16),
                  w_br_c.astype(BF16), w_out.astype(BF16), g, b, alpha)


def kernel(x, positions, w_in, sink_b, w_br_a, w_br_b, w_br_c, w_out,
           ffn1_in, ffn1_out, ffn2_in, ffn2_out, ln_g, ln_b):
    batch, seq, d = x.shape
    depth = w_in.shape[0]
    alpha = float((2 * depth) ** 0.25)
    cos, sin = _rope_tables(positions)
    h = x.reshape(batch * seq, d)
    for l in range(depth):
        gb = lambda k: (ln_g[l, k][None, :], ln_b[l, k][None, :])
        h = _ffn(h, ffn1_in[l].astype(BF16), ffn1_out[l].astype(BF16), *gb(0), alpha)
        h = _mixer(h, cos, sin, w_in[l], sink_b[l], w_br_a[l], w_br_b[l], w_br_c[l], w_out[l],
                   *gb(1), alpha, batch, seq)
        h = _ffn(h, ffn2_in[l].astype(BF16), ffn2_out[l].astype(BF16), *gb(2), alpha)
    return h.reshape(batch, seq, d)
```

```python
import functools

import jax
import jax.numpy as jnp
import numpy as np
from jax import lax
from jax.experimental import pallas as pl
from jax.experimental.pallas import tpu as pltpu

HEAD_DIM = 64
ROPE_THETA = 10000.0
A_HEADS = 8
A_CONFIGS = ((128, 1), (512, 4), (2048, 16))
B_HEADS = 8
B_KV_HEADS = 2
B_WINDOW = 128
C_HEADS = 8
IDX_HEADS = 8
TOPK_MAX = 256
BLOCK = 128
LN_EPS = 1e-5
LANES = 128
VMEM_LIMIT_BYTES = 56 * 1024 * 1024
INT_MIN = -(2 ** 31)
I16_MIN = -(2 ** 15)
LOG2E = 1.4426950408889634

F32 = jnp.float32
BF16 = jnp.bfloat16
I16 = jnp.int16
I32 = jnp.int32


def _params(*sem):
    return pltpu.CompilerParams(dimension_semantics=sem, vmem_limit_bytes=VMEM_LIMIT_BYTES)


def _layer_norm(y, g, b):
    mu = jnp.mean(y, axis=-1, keepdims=True)
    d = y - mu
    var = jnp.mean(d * d, axis=-1, keepdims=True)
    return d * lax.rsqrt(var + LN_EPS) * g + b


def _rope_table_kernel(pos_ref, inv_ref, sign_ref, cos_ref, sin_ref):
    ang = pos_ref[...].astype(F32) * inv_ref[...]
    cos_ref[...] = jnp.cos(ang)
    sin_ref[...] = jnp.sin(ang) * sign_ref[...]


def _rope_tables(positions):
    n = positions.size
    tm = 1024
    inv = ROPE_THETA ** (-jnp.arange(0, HEAD_DIM, 2, dtype=F32) / HEAD_DIM)
    inv = jnp.tile(inv, LANES // inv.shape[0])[None, :]
    half = (np.arange(LANES) % HEAD_DIM) < HEAD_DIM // 2
    sign = jnp.asarray(np.where(half, -1.0, 1.0), F32)[None, :]
    row = pl.BlockSpec((tm, LANES), lambda i: (i, 0))
    const = pl.BlockSpec((1, LANES), lambda i: (0, 0))
    return pl.pallas_call(
        _rope_table_kernel,
        grid=(n // tm,),
        in_specs=[pl.BlockSpec((tm, 1), lambda i: (i, 0)), const, const],
        out_specs=[row, row],
        out_shape=[jax.ShapeDtypeStruct((n, LANES), F32)] * 2,
        compiler_params=_params("parallel"),
        name="rope_tables",
    )(positions.reshape(n, 1), inv, sign)


FFN_SUB = 256


def _ffn_kernel(x_ref, wg_ref, wu_ref, wo_ref, g_ref, b_ref, o_ref, *, alpha):
    for r0 in range(0, x_ref.shape[0], FFN_SUB):
        x = x_ref[r0:r0 + FFN_SUB, :]
        xb = x.astype(BF16)
        gate = jnp.dot(xb, wg_ref[...], preferred_element_type=F32)
        up = jnp.dot(xb, wu_ref[...], preferred_element_type=F32)
        h = (gate * jax.nn.sigmoid(gate) * up).astype(BF16)
        y = alpha * x + 0.5 * jnp.dot(h, wo_ref[...], preferred_element_type=F32)
        o_ref[r0:r0 + FFN_SUB, :] = _layer_norm(y, g_ref[...], b_ref[...])


def _resident(shape, index_map):
    return pl.BlockSpec(shape, index_map, pipeline_mode=pl.Buffered(1))


def _ffn(x, w_in, w_out, g, b, alpha):
    n, d = x.shape
    d_ff = w_out.shape[0]
    tm = 512
    return pl.pallas_call(
        functools.partial(_ffn_kernel, alpha=alpha),
        grid=(n // tm,),
        in_specs=[
            pl.BlockSpec((tm, d), lambda i: (i, 0)),
            _resident((d, d_ff), lambda i: (0, 0)),
            _resident((d, d_ff), lambda i: (0, 1)),
            _resident((d_ff, d), lambda i: (0, 0)),
            _resident((1, d), lambda i: (0, 0)),
            _resident((1, d), lambda i: (0, 0)),
        ],
        out_specs=pl.BlockSpec((tm, d), lambda i: (i, 0)),
        out_shape=jax.ShapeDtypeStruct((n, d), F32),
        compiler_params=_params("parallel"),
        name="ffn_ln",
    )(x, w_in, w_in, w_out, g, b)


A_W = A_HEADS * HEAD_DIM
PACK_A = 3 * A_W
PACK_B = B_HEADS * HEAD_DIM + 4 * LANES
PACK_C = 2 * C_HEADS * HEAD_DIM + 3 * LANES
C_KC, C_KI, C_VC = 1024, 1152, 1280


def _pack_in_weights(w):
    sizes = (A_W, A_W, A_W, 512, 128, 128, 512, 64, 64, 512, 64, 8, 1024, 1024, 1024)
    offs = np.concatenate([[0], np.cumsum(sizes)])
    (qa, ka, va, qb, kb, vb, qc, kc, vc, qi, ki, wi, ga, gb, gc) = [
        w[:, offs[i]:offs[i + 1]] for i in range(len(sizes))]

    def dup_heads(t):
        g = t.shape[1] // HEAD_DIM
        t = t.reshape(t.shape[0], g, 1, HEAD_DIM)
        return jnp.broadcast_to(t, (t.shape[0], g, 2, HEAD_DIM)).reshape(t.shape[0], g * LANES)

    wi_pad = jnp.pad(wi, ((0, 0), (0, LANES - wi.shape[1])))
    packed = jnp.concatenate(
        [qa, ka, va, qb, dup_heads(kb), dup_heads(vb),
         qc, qi, dup_heads(kc), dup_heads(ki), dup_heads(vc), wi_pad], axis=1)
    gates = jnp.concatenate([ga, gb, gc], axis=1)
    return packed.astype(BF16), gates.astype(BF16)


def _rope(h, cos, sin):
    w = h.shape[1]
    reps = w // LANES
    c = jnp.tile(cos, (1, reps)) if reps > 1 else cos
    s = jnp.tile(sin, (1, reps)) if reps > 1 else sin
    lane = lax.broadcasted_iota(jnp.int32, h.shape, 1)
    first = (lane % HEAD_DIM) < HEAD_DIM // 2
    partner = jnp.where(first, pltpu.roll(h, w - HEAD_DIM // 2, 1), pltpu.roll(h, HEAD_DIM // 2, 1))
    return h * c + partner * s


def _inproj_kernel(x_ref, w_ref, cos_ref, sin_ref, a_ref, b_ref, c_ref, wi_ref):
    xb = x_ref[...].astype(BF16)
    cos = cos_ref[...]
    sin = sin_ref[...]
    scale = HEAD_DIM ** -0.5

    def proj(c0, width):
        return jnp.dot(xb, w_ref[:, c0:c0 + width], preferred_element_type=F32)

    a_ref[:, 0:512] = _rope(proj(0, 512), cos, sin) * scale
    a_ref[:, 512:1024] = _rope(proj(512, 512), cos, sin)
    a_ref[:, 1024:1536] = proj(1024, 512)
    o = PACK_A
    b_ref[:, 0:512] = (_rope(proj(o, 512), cos, sin) * scale).astype(BF16)
    b_ref[:, 512:768] = _rope(proj(o + 512, 256), cos, sin).astype(BF16)
    b_ref[:, 768:1024] = proj(o + 768, 256).astype(BF16)
    o = PACK_A + PACK_B
    c_ref[:, 0:512] = (_rope(proj(o, 512), cos, sin) * (scale * LOG2E)).astype(BF16)
    c_ref[:, 512:1024] = _rope(proj(o + 512, 512), cos, sin).astype(BF16)
    c_ref[:, 1024:1280] = _rope(proj(o + 1024, 256), cos, sin).astype(BF16)
    lane = lax.broadcasted_iota(I32, (xb.shape[0], LANES), 1)
    c_ref[:, 1280:1408] = jnp.where(lane < HEAD_DIM, proj(o + 1280, 128), 1.0).astype(BF16)
    wi_ref[...] = proj(o + 1408, 128)


def _inproj(x, w_packed, cos, sin):
    n, d = x.shape
    tm = 512
    wtot = w_packed.shape[1]
    row = lambda w: pl.BlockSpec((tm, w), lambda i: (i, 0))
    return pl.pallas_call(
        _inproj_kernel,
        grid=(n // tm,),
        in_specs=[row(d), pl.BlockSpec((d, wtot), lambda i: (0, 0)), row(LANES), row(LANES)],
        out_specs=[row(PACK_A), row(PACK_B), row(PACK_C), row(LANES)],
        out_shape=[
            jax.ShapeDtypeStruct((n, PACK_A), F32),
            jax.ShapeDtypeStruct((n, PACK_B), BF16),
            jax.ShapeDtypeStruct((n, PACK_C), BF16),
            jax.ShapeDtypeStruct((n, LANES), F32),
        ],
        compiler_params=_params("parallel"),
        name="inproj_rope",
    )(x, w_packed, cos, sin)


def _band_kernel(*refs, configs, seq, has_sink):
    q_ref, k_ref, v_ref = refs[:3]
    rest = refs[3:]
    if has_sink:
        sink_ref, rest = rest[0], rest[1:]
    o_ref, qs_ref, kf_ref, vf_ref = rest[:4]
    multi = len(configs) > 1
    if multi:
        num_ref, m_ref, den_ref = rest[4:7]

    lane = lax.broadcasted_iota(I32, (1, LANES), 1)
    lo = lane < HEAD_DIM
    if has_sink:
        sk = jnp.max(jnp.concatenate([jnp.broadcast_to(sink_ref[0, 0:1, :], (BLOCK, LANES)),
                                      jnp.broadcast_to(sink_ref[0, 1:2, :], (BLOCK, LANES))], axis=0),
                     axis=-1, keepdims=True)

    def valid_mask(nk, max_dist):
        qi = lax.broadcasted_iota(I32, (BLOCK, nk), 0)
        kj = lax.broadcasted_iota(I32, (BLOCK, nk), 1)
        dist = qi + (nk - BLOCK) - kj
        ok = (dist >= 0) & (dist <= max_dist)
        return jnp.concatenate([ok, ok], axis=0)

    def rows(ref, start, size, stride):
        if stride == 1:
            return ref[0, start:start + size, :]
        return ref[0, pl.ds(start, size, stride=stride), :]

    for c, (dil, max_dist) in enumerate(configs):
        fl = seq // dil
        for r in range(dil):
            dst = slice(r * fl, (r + 1) * fl)
            x = rows(q_ref, r, fl, dil).astype(F32)
            qs_ref[0, dst, :] = jnp.where(lo, x, 0.0).astype(BF16)
            qs_ref[1, dst, :] = jnp.where(lo, 0.0, x).astype(BF16)
            kf_ref[dst, :] = rows(k_ref, r, fl, dil).astype(BF16)
            vf_ref[dst, :] = rows(v_ref, r, fl, dil).astype(BF16)
        valid_full = valid_mask(2 * BLOCK, max_dist)
        valid_first = valid_mask(BLOCK, max_dist)
        for r in range(dil):
            for jb in range(fl // BLOCK):
                r0 = r * fl + jb * BLOCK
                k0 = r0 - BLOCK if jb > 0 else r0
                valid = valid_full if jb > 0 else valid_first
                q2 = jnp.concatenate([qs_ref[0, r0:r0 + BLOCK, :], qs_ref[1, r0:r0 + BLOCK, :]], axis=0)
                s = lax.dot_general(q2, kf_ref[k0:r0 + BLOCK, :], (((1,), (1,)), ((), ())),
                                    preferred_element_type=F32)
                s = jnp.where(valid, s, -jnp.inf)
                m = jnp.max(s, axis=-1, keepdims=True)
                if has_sink:
                    m = jnp.maximum(m, sk)
                p = jnp.exp(s - m)
                den = jnp.sum(p, axis=-1, keepdims=True)
                if has_sink:
                    den = den + jnp.exp(sk - m)
                pv = jnp.dot(p.astype(BF16), vf_ref[k0:r0 + BLOCK, :], preferred_element_type=F32)
                num = jnp.where(lo, pv[:BLOCK], pv[BLOCK:])
                den2 = jnp.where(lo, den[:BLOCK], den[BLOCK:])
                t0 = r + dil * BLOCK * jb
                if not multi:
                    o_ref[0, t0:t0 + BLOCK, :] = (num / den2).astype(o_ref.dtype)
                    continue
                m2 = jnp.where(lo, m[:BLOCK], m[BLOCK:])
                for ref, val in ((num_ref, num), (m_ref, m2), (den_ref, den2)):
                    if dil == 1:
                        ref[c, t0:t0 + BLOCK, :] = val
                    else:
                        ref[c, pl.ds(t0, BLOCK, stride=dil), :] = val

    if multi:
        step = 2 * BLOCK
        for t0 in range(0, seq, step):
            sl = slice(t0, t0 + step)
            ms = [m_ref[c, sl, :] for c in range(len(configs))]
            top = functools.reduce(jnp.maximum, ms)
            ws = [jnp.exp(mc - top) for mc in ms]
            numer = sum(w * num_ref[c, sl, :] for c, w in enumerate(ws))
            denom = sum(w * den_ref[c, sl, :] for c, w in enumerate(ws))
            o_ref[0, sl, :] = (numer / denom).astype(o_ref.dtype)


def _band_attention(qkv, *, batch, seq, q_col, k_col, v_col, kv_tile_of, configs, sink=None):
    n_tiles = 4
    qkv = qkv.reshape(batch, seq, qkv.shape[-1])
    tile = lambda col_of: pl.BlockSpec((1, seq, LANES), lambda b, t: (b, 0, col_of(t)))
    in_specs = [tile(lambda t: q_col // LANES + t),
                tile(lambda t: k_col // LANES + kv_tile_of(t)),
                tile(lambda t: v_col // LANES + kv_tile_of(t))]
    args = [qkv] * 3
    if sink is not None:
        sink_tiles = jnp.broadcast_to(sink.astype(F32).reshape(n_tiles, 2, 1), (n_tiles, 2, LANES))
        in_specs.append(pl.BlockSpec((1, 2, LANES), lambda b, t: (t, 0, 0)))
        args.append(sink_tiles)
    scratch = [pltpu.VMEM((2, seq, LANES), BF16), pltpu.VMEM((seq, LANES), BF16),
               pltpu.VMEM((seq, LANES), BF16)]
    if len(configs) > 1:
        scratch += [pltpu.VMEM((len(configs), seq, LANES), F32)] * 3
    out = pl.pallas_call(
        functools.partial(_band_kernel, configs=configs, seq=seq, has_sink=sink is not None),
        grid=(batch, n_tiles),
        in_specs=in_specs,
        out_specs=pl.BlockSpec((1, seq, LANES), lambda b, t: (b, 0, t)),
        out_shape=jax.ShapeDtypeStruct((batch, seq, n_tiles * LANES), BF16),
        scratch_shapes=scratch,
        compiler_params=_params("parallel", "arbitrary"),
        name="band_attention",
    )(*args)
    return out.reshape(batch * seq, n_tiles * LANES)


DSA_CHUNK = 2 * BLOCK
BOUND_SLACK = 1.03
DEN_FLOOR = 2.0 ** -60
NEG_BIG = -(2.0 ** 100)


def _fold_rows(x, rows):
    parts = [x[r:r + rows] for r in range(0, x.shape[0], rows)]
    while len(parts) > 1:
        parts = [a + b for a, b in zip(parts[::2], parts[1::2])] + (parts[-1:] if len(parts) % 2 else [])
    return parts[0]


def _dsa_kernel(qc0_ref, qc1_ref, qi0_ref, qi1_ref, kc_ref, ki_ref, vc_ref, wi0_ref, wi1_ref, o_ref,
                qsc_ref, qsi_ref, wt_ref, key_ref, hi_ref, lo_ref, acc_ref, *, k_sel, n_tiles):
    j = pl.program_id(1)
    n_heads = C_HEADS
    ch = DSA_CHUNK
    n_chunks = (n_tiles * BLOCK) // ch + 1
    wide = n_heads * LANES
    tile_of = (j, n_tiles - 1 - j)
    n0 = (j * BLOCK + BLOCK + ch - 1) // ch
    lane = lax.broadcasted_iota(I32, (1, LANES), 1)
    lo_lanes = lane < HEAD_DIM

    for slot, (qc_ref, qi_ref, wi_ref) in enumerate(((qc0_ref, qi0_ref, wi0_ref),
                                                     (qc1_ref, qi1_ref, wi1_ref))):
        for h in range(n_heads):
            t = h // 2
            keep = lo_lanes if h % 2 == 0 else jnp.logical_not(lo_lanes)
            for src, dst in ((qc_ref, qsc_ref), (qi_ref, qsi_ref)):
                tile = src[:, t * LANES:(t + 1) * LANES].astype(F32)
                dst[slot, h * LANES:(h + 1) * LANES, :] = jnp.where(keep, tile, 0.0).astype(BF16)
        wt_ref[slot] = wi_ref[...].T

    def chunk_info(c):
        is1 = c >= n0
        slot = jnp.where(is1, 1, 0)
        k0 = pl.multiple_of((c - jnp.where(is1, n0, 0)) * ch, ch)
        q0 = jnp.where(is1, tile_of[1], tile_of[0]) * BLOCK
        return is1, slot, k0, q0

    info = [chunk_info(c) for c in range(n_chunks)]
    pick = lambda is1, a: jnp.where(is1, a[1], a[0])

    row_minus_col = (lax.broadcasted_iota(I32, (ch, LANES), 0)
                     - lax.broadcasted_iota(I32, (ch, LANES), 1))

    for c, (is1, slot, k0, q0) in enumerate(info):
        d = lax.dot_general(ki_ref[pl.ds(k0, ch), :], qsi_ref[slot],
                            (((1,), (1,)), ((), ())), preferred_element_type=F32)
        sc = jnp.zeros((ch, LANES), F32)
        for h in range(IDX_HEADS):
            sc = sc + jnp.maximum(d[:, h * LANES:(h + 1) * LANES], 0.0) * wt_ref[slot, h:h + 1, :]
        bits = pltpu.bitcast(sc, I32)
        key = bits ^ ((bits >> 31) & jnp.int32(0x7FFFFFFF))
        key = jnp.where(sc == 0.0, 0, key)
        key = jnp.where(row_minus_col <= q0 - k0, key, INT_MIN)
        key_ref[c] = key
        hi_ref[c] = (key >> 16).astype(I16)
        lo_ref[c] = ((key & 0xFFFF) - 32768).astype(I16)

    def count16(ref, cands, strict=False):
        acc = [jnp.zeros((16, LANES), I16), jnp.zeros((16, LANES), I16)]
        for c, (is1, _, _, _) in enumerate(info):
            cand = pick(is1, cands)
            blk = ref[c]
            hit = jnp.where(blk > cand if strict else blk >= cand, jnp.int16(1), jnp.int16(0))
            part = _fold_rows(hit, 16)
            zero = jnp.zeros_like(part)
            acc[0] = acc[0] + jnp.where(is1, zero, part)
            acc[1] = acc[1] + jnp.where(is1, part, zero)
        return [jnp.sum(a.astype(I32), axis=0, keepdims=True) for a in acc]

    def search16(ref, need):
        def step(it, tu):
            bit = lax.shift_left(jnp.int32(1), 15 - it)
            cu = [tu[0] | bit, tu[1] | bit]
            cnt = count16(ref, [(u - 32768).astype(I16) for u in cu])
            return tuple(jnp.where(cnt[s] >= need[s], cu[s], tu[s]) for s in range(2))
        z = jnp.zeros((1, LANES), I32)
        tu = lax.fori_loop(0, 16, step, (z, z))
        return [u - 32768 for u in tu]

    ksel = jnp.full((1, LANES), k_sel, I32)
    p1 = search16(hi_ref, [ksel, ksel])
    p1_16 = [p.astype(I16) for p in p1]
    above = count16(hi_ref, p1_16, strict=True)
    for c, (is1, _, _, _) in enumerate(info):
        lo_ref[c] = jnp.where(hi_ref[c] == pick(is1, p1_16), lo_ref[c], jnp.int16(I16_MIN))
    p2 = search16(lo_ref, [ksel - above[0], ksel - above[1]])
    thr = [jnp.maximum((p1[s] << 16) | ((p2[s] + 32768) & 0xFFFF), INT_MIN + 1) for s in range(2)]

    def count32(pred):
        acc = [jnp.zeros((8, LANES), I32), jnp.zeros((8, LANES), I32)]
        for c, (is1, _, _, _) in enumerate(info):
            hit = jnp.where(pred(key_ref[c], pick(is1, thr)), 1, 0)
            part = jnp.sum(hit.reshape(ch // 8, 8, LANES), axis=0)
            zero = jnp.zeros_like(part)
            acc[0] = acc[0] + jnp.where(is1, zero, part)
            acc[1] = acc[1] + jnp.where(is1, part, zero)
        return [jnp.sum(a, axis=0, keepdims=True) for a in acc]

    c_ge = count32(lambda k, t: k >= t)
    over = jnp.max(jnp.maximum(c_ge[0], c_ge[1])) > k_sel

    @pl.when(over)
    def _():
        c_gt = count32(lambda k, t: k > t)
        need = [(k_sel - c).astype(F32) for c in c_gt]
        tri = (lax.broadcasted_iota(I32, (ch, ch), 1)
               <= lax.broadcasted_iota(I32, (ch, ch), 0)).astype(BF16)
        carry = jnp.zeros((1, LANES), F32)
        for c, (is1, _, _, _) in enumerate(info):
            carry = jnp.where(n0 == c, 0.0, carry)
            key = key_ref[c]
            eq = jnp.where(key == pick(is1, thr), 1.0, 0.0)
            pre = jnp.dot(tri, eq.astype(BF16), preferred_element_type=F32) + carry
            drop = (eq > 0.0) & (pre > pick(is1, need))
            key_ref[c] = jnp.where(drop, INT_MIN, key)
            carry = carry + jnp.sum(eq, axis=0, keepdims=True)

    def logits(c):
        _, slot, k0, _ = info[c]
        return lax.dot_general(kc_ref[pl.ds(k0, ch), :], qsc_ref[slot],
                               (((1,), (1,)), ((), ())), preferred_element_type=F32)

    def selected(c):
        return jnp.tile(key_ref[c] >= pick(info[c][0], thr), (1, n_heads))

    def weighted_values(c, p):
        return lax.dot_general(vc_ref[pl.ds(info[c][2], ch), :], p.astype(BF16),
                               (((0,), (0,)), ((), ())), preferred_element_type=F32)

    ksq = kc_ref[...].astype(F32)
    ksq = 0.5 * jnp.max(jnp.sum(ksq * ksq, axis=1, keepdims=True), axis=0, keepdims=True)
    ones8 = jnp.ones((8, LANES), BF16)
    bound = []
    for slot in range(2):
        q2 = qsc_ref[slot].astype(F32)
        qsq = lax.dot_general(ones8, (q2 * q2).astype(BF16), (((1,), (1,)), ((), ())),
                              preferred_element_type=F32)[0:1, :]
        bound.append(jnp.sqrt(qsq * ksq) * BOUND_SLACK)
    acc_ref[...] = jnp.zeros_like(acc_ref)
    for c, (is1, slot, _, _) in enumerate(info):
        p = jnp.where(selected(c), jnp.exp2(logits(c) - pick(is1, bound)), 0.0)
        acc_ref[slot] += weighted_values(c, p)
    den_min = jnp.min(jnp.minimum(acc_ref[0, HEAD_DIM:HEAD_DIM + 1, :], acc_ref[1, HEAD_DIM:HEAD_DIM + 1, :]))

    @pl.when(jnp.logical_not(den_min > DEN_FLOOR))
    def _():
        acc_ref[...] = jnp.zeros_like(acc_ref)
        m = [jnp.full((1, wide), -jnp.inf, F32)] * 2
        for c, (is1, slot, _, _) in enumerate(info):
            s = jnp.where(selected(c), logits(c), -jnp.inf)
            m_old = pick(is1, m)
            m_new = jnp.maximum(m_old, jnp.max(s, axis=0, keepdims=True))
            m_safe = jnp.where(m_new == -jnp.inf, 0.0, m_new)
            acc_ref[slot] = acc_ref[slot] * jnp.exp2(m_old - m_safe) + weighted_values(c, jnp.exp2(s - m_safe))
            m = [jnp.where(is1, m[0], m_new), jnp.where(is1, m_new, m[1])]

    lo2 = lax.broadcasted_iota(I32, (BLOCK, LANES), 1) < HEAD_DIM
    for slot in range(2):
        acc = acc_ref[slot]
        o_t = acc / acc[HEAD_DIM:HEAD_DIM + 1, :]
        for t in range(n_heads // 2):
            ev = o_t[:, (2 * t) * LANES:(2 * t + 1) * LANES].T
            od = o_t[:, (2 * t + 1) * LANES:(2 * t + 2) * LANES].T
            o_ref[0, tile_of[slot], :, t * LANES:(t + 1) * LANES] = (
                jnp.where(lo2, ev, pltpu.roll(od, HEAD_DIM, 1)).astype(o_ref.dtype))


def _dsa(c_pack, wi, batch, seq):
    n = batch * seq
    nq = seq // BLOCK
    assert nq % 2 == 0 and (nq * BLOCK) % DSA_CHUNK == 0
    n_chunks = (nq * BLOCK) // DSA_CHUNK + 1
    k_sel = min(TOPK_MAX, seq // 4)
    qw = C_HEADS * HEAD_DIM
    first = lambda b, j: b * nq + j
    second = lambda b, j: b * nq + nq - 1 - j
    qrow = lambda blk, which: pl.BlockSpec((BLOCK, qw), lambda b, j: (which(b, j), blk))
    wrow = lambda which: pl.BlockSpec((BLOCK, LANES), lambda b, j: (which(b, j), 0))
    full = lambda col: pl.BlockSpec((seq, LANES), lambda b, j: (b, col // LANES))
    out = pl.pallas_call(
        functools.partial(_dsa_kernel, k_sel=k_sel, n_tiles=nq),
        grid=(batch, nq // 2),
        in_specs=[qrow(0, first), qrow(0, second), qrow(1, first), qrow(1, second),
                  full(C_KC), full(C_KI), full(C_VC), wrow(first), wrow(second)],
        out_specs=pl.BlockSpec((1, nq, BLOCK, qw), lambda b, j: (b, 0, 0, 0)),
        out_shape=jax.ShapeDtypeStruct((batch, nq, BLOCK, qw), BF16),
        scratch_shapes=[
            pltpu.VMEM((2, C_HEADS * LANES, LANES), BF16),
            pltpu.VMEM((2, IDX_HEADS * LANES, LANES), BF16),
            pltpu.VMEM((2, LANES, LANES), F32),
            pltpu.VMEM((n_chunks, DSA_CHUNK, LANES), I32),
            pltpu.VMEM((n_chunks, DSA_CHUNK, LANES), I16),
            pltpu.VMEM((n_chunks, DSA_CHUNK, LANES), I16),
            pltpu.VMEM((2, LANES, C_HEADS * LANES), F32),
        ],
        compiler_params=_params("parallel", "arbitrary"),
        name="dsa",
    )(c_pack, c_pack, c_pack, c_pack, c_pack, c_pack, c_pack, wi, wi)
    return out.reshape(n, qw)


def _dsa_pipe_kernel(qc0_ref, qc1_ref, qi0_ref, qi1_ref, wi0_ref, wi1_ref, ki_ref, kc_ref, vc_ref, o_ref,
                     qsc_ref, qsi_ref, wt_ref, key_ref, hi_ref, lo_ref, thr_ref, acc_ref, s_ref,
                     *, k_sel, n_tiles, n_steps):
    t = pl.program_id(0)
    n_heads = C_HEADS
    ch = DSA_CHUNK
    pairs = n_tiles // 2
    n_chunks = (n_tiles * BLOCK) // ch + 1
    max_n0 = (pairs * BLOCK + ch - 1) // ch
    wide = n_heads * LANES
    par1 = t % 2
    par2 = 1 - par1
    j1 = jnp.minimum(t, n_steps - 2) % pairs
    j2 = jnp.maximum(t - 1, 0) % pairs

    @pl.when(t == 0)
    def _():
        qsc_ref[1] = jnp.zeros(qsc_ref.shape[1:], BF16)
        eye = (lax.broadcasted_iota(I32, (LANES, LANES), 0)
               == lax.broadcasted_iota(I32, (LANES, LANES), 1)).astype(F32).astype(BF16)
        for par in range(2):
            for slot in range(2):
                for h in range(n_heads):
                    qsc_ref[par, slot, h * LANES:(h + 1) * LANES, LANES:2 * LANES] = eye
        key_ref[1] = jnp.zeros(key_ref.shape[1:], I32)
        thr_ref[1] = jnp.zeros(thr_ref.shape[1:], I32)

    def chunk_infos(j):
        n0 = (j * BLOCK + BLOCK + ch - 1) // ch
        tiles = (j, n_tiles - 1 - j)
        out = []
        for c in range(n_chunks):
            if c == 0:
                is1, slot, base = False, 0, 0
            elif c >= max_n0:
                is1, slot, base = True, 1, n0
            else:
                is1 = c >= n0
                slot, base = jnp.where(is1, 1, 0), jnp.where(is1, n0, 0)
            k0 = pl.multiple_of((c - base) * ch, ch)
            q0 = (tiles[slot] if isinstance(is1, bool) else jnp.where(is1, tiles[1], tiles[0])) * BLOCK
            out.append((is1, slot, k0, q0))
        return n0, tiles, out

    def pick(is1, a):
        return a[int(is1)] if isinstance(is1, bool) else jnp.where(is1, a[1], a[0])

    def add_to_slot(acc, is1, part):
        if isinstance(is1, bool):
            acc[int(is1)] = acc[int(is1)] + part
        else:
            zero = jnp.zeros_like(part)
            acc[0] = acc[0] + jnp.where(is1, zero, part)
            acc[1] = acc[1] + jnp.where(is1, part, zero)

    lane = lax.broadcasted_iota(I32, (1, LANES), 1)
    head_mask = [jnp.where(lane < HEAD_DIM, 1.0, 0.0).astype(BF16),
                 jnp.where(lane < HEAD_DIM, 0.0, 1.0).astype(BF16)]

    n0_2, tiles2, info2 = chunk_infos(j2)
    thr2 = [thr_ref[par2, s, 0:1, :] for s in range(2)]

    def logits(c):
        _, slot, k0, _ = info2[c]
        return lax.dot_general(kc_ref[pl.ds(k0, ch), :], qsc_ref[par2, slot, :, 0:LANES],
                               (((1,), (1,)), ((), ())), preferred_element_type=F32)

    def selected(c):
        return jnp.tile(key_ref[par2, c] >= pick(info2[c][0], thr2), (1, n_heads))

    def weighted_values(c, p):
        return lax.dot_general(vc_ref[pl.ds(info2[c][2], ch), :], p.astype(BF16),
                               (((0,), (0,)), ((), ())), preferred_element_type=F32)

    ksq = kc_ref[...].astype(F32)
    ksq = 0.5 * jnp.max(jnp.sum(ksq * ksq, axis=1, keepdims=True), axis=0, keepdims=True)
    ones8 = jnp.ones((8, LANES), BF16)
    bound = []
    for slot in range(2):
        q2 = qsc_ref[par2, slot, :, 0:LANES].astype(F32)
        qsq = lax.dot_general(ones8, (q2 * q2).astype(BF16), (((1,), (1,)), ((), ())),
                              preferred_element_type=F32)[0:1, :]
        per_head = jnp.sqrt(qsq * ksq) * BOUND_SLACK
        bound.append(functools.reduce(
            jnp.maximum, [per_head[:, h * LANES:(h + 1) * LANES] for h in range(n_heads)]))
    acc_ref[...] = jnp.zeros_like(acc_ref)

    def chunk_place(c):
        is1 = c >= n0_2
        return is1, jnp.where(is1, 1, 0), pl.multiple_of((c - jnp.where(is1, n0_2, 0)) * ch, ch)

    def logits_stage(c):
        is1, slot, k0 = chunk_place(c)
        sel = key_ref[par2, c] >= jnp.where(is1, thr2[1], thr2[0])
        bias = jnp.where(sel, -jnp.where(is1, bound[1], bound[0]), NEG_BIG).astype(BF16)
        lhs = jnp.concatenate([kc_ref[pl.ds(k0, ch), :], bias], axis=1)
        s_ref[c % 2] = lax.dot_general(lhs, qsc_ref[par2, slot], (((1,), (1,)), ((), ())),
                                       preferred_element_type=F32)

    def values_stage(c):
        _, slot, k0 = chunk_place(c)
        acc_ref[slot] += lax.dot_general(vc_ref[pl.ds(k0, ch), :], jnp.exp2(s_ref[c % 2]).astype(BF16),
                                         (((0,), (0,)), ((), ())), preferred_element_type=F32)

    def attention_chunk(c):
        values_stage(c)
        nxt = jnp.minimum(c + 1, n_chunks - 1)
        logits_stage(nxt)
        never = lambda x: jnp.where((pltpu.bitcast(x, I32) | 1) == 0, 1, 0)
        return never(acc_ref[0, 0:1, 0:LANES]), never(s_ref[nxt % 2, 0:1, 0:LANES])

    logits_stage(0)

    n0_1, tiles1, info1 = chunk_infos(j1)
    for slot, (qc_ref, qi_ref, wi_ref) in enumerate(((qc0_ref, qi0_ref, wi0_ref),
                                                     (qc1_ref, qi1_ref, wi1_ref))):
        for h in range(n_heads):
            tl = slice((h // 2) * LANES, (h // 2 + 1) * LANES)
            rows_h = slice(h * LANES, (h + 1) * LANES)
            qsc_ref[par1, slot, rows_h, 0:LANES] = qc_ref[:, tl] * head_mask[h % 2]
            qsi_ref[slot, rows_h, :] = qi_ref[:, tl] * head_mask[h % 2]
        wt_ref[slot] = wi_ref[...].T

    row_minus_col = (lax.broadcasted_iota(I32, (ch, LANES), 0)
                     - lax.broadcasted_iota(I32, (ch, LANES), 1))
    for c, (is1, slot, k0, q0) in enumerate(info1):
        d = lax.dot_general(ki_ref[pl.ds(k0, ch), :], qsi_ref[slot],
                            (((1,), (1,)), ((), ())), preferred_element_type=F32)
        sc = jnp.zeros((ch, LANES), F32)
        for h in range(IDX_HEADS):
            sc = sc + jnp.maximum(d[:, h * LANES:(h + 1) * LANES], 0.0) * wt_ref[slot, h:h + 1, :]
        bits = pltpu.bitcast(sc, I32)
        key = bits ^ ((bits >> 31) & jnp.int32(0x7FFFFFFF))
        key = jnp.where(sc == 0.0, 0, key)
        key = jnp.where(row_minus_col <= q0 - k0, key, INT_MIN)
        key_ref[par1, c] = key
        hi_ref[c] = (key >> 16).astype(I16)
        lo_ref[c] = ((key & 0xFFFF) - 32768).astype(I16)

    def count16(ref, cands, strict=False):
        acc = [jnp.zeros((16, LANES), I16), jnp.zeros((16, LANES), I16)]
        for c, (is1, _, _, _) in enumerate(info1):
            cand = pick(is1, cands)
            blk = ref[c]
            hit = jnp.where(blk > cand if strict else blk >= cand, jnp.int16(1), jnp.int16(0))
            add_to_slot(acc, is1, _fold_rows(hit, 16))
        return [jnp.sum(a.astype(I32), axis=0, keepdims=True) for a in acc]

    bits_per_trip = 4

    def search16(ref, need, first_chunk, trips):
        assert trips * bits_per_trip >= 16

        def trip(i, tu):
            zero_mid, zero_end = attention_chunk(first_chunk + i)
            for b in range(bits_per_trip):
                it = i * bits_per_trip + b
                bit = jnp.where(it < 16, lax.shift_left(jnp.int32(1), jnp.maximum(15 - it, 0)), 0)
                cu = [u | bit for u in tu]
                cnt = count16(ref, [(u - 32768).astype(I16) for u in cu])
                tu = tuple(jnp.where(cnt[s] >= need[s], cu[s], tu[s]) for s in range(2))
            return tuple(u | zero_mid | zero_end for u in tu)

        z = jnp.zeros((1, LANES), I32)
        tu = lax.fori_loop(0, trips, trip, (z, z))
        return [u - 32768 for u in tu]

    trips_hi = (n_chunks + 1) // 2
    ksel = jnp.full((1, LANES), k_sel, I32)
    p1 = search16(hi_ref, [ksel, ksel], 0, trips_hi)
    p1_16 = [p.astype(I16) for p in p1]
    above = count16(hi_ref, p1_16, strict=True)
    for c, (is1, _, _, _) in enumerate(info1):
        lo_ref[c] = jnp.where(hi_ref[c] == pick(is1, p1_16), lo_ref[c], jnp.int16(I16_MIN))
    p2 = search16(lo_ref, [ksel - above[0], ksel - above[1]], trips_hi, n_chunks - trips_hi)
    thr1 = [jnp.maximum((p1[s] << 16) | ((p2[s] + 32768) & 0xFFFF), INT_MIN + 1) for s in range(2)]
    for s in range(2):
        thr_ref[par1, s] = jnp.broadcast_to(thr1[s], (8, LANES))

    def count32(pred):
        acc = [jnp.zeros((8, LANES), I32), jnp.zeros((8, LANES), I32)]
        for c, (is1, _, _, _) in enumerate(info1):
            hit = jnp.where(pred(key_ref[par1, c], pick(is1, thr1)), 1, 0)
            add_to_slot(acc, is1, jnp.sum(hit.reshape(ch // 8, 8, LANES), axis=0))
        return [jnp.sum(a, axis=0, keepdims=True) for a in acc]

    c_ge = count32(lambda k, th: k >= th)
    over = jnp.max(jnp.maximum(c_ge[0], c_ge[1])) > k_sel
    den_min = jnp.min(jnp.minimum(acc_ref[0, HEAD_DIM:HEAD_DIM + 1, :],
                                  acc_ref[1, HEAD_DIM:HEAD_DIM + 1, :]))

    @pl.when(over)
    def _():
        c_gt = count32(lambda k, th: k > th)
        need = [(k_sel - c).astype(F32) for c in c_gt]
        tri = (lax.broadcasted_iota(I32, (ch, ch), 1)
               <= lax.broadcasted_iota(I32, (ch, ch), 0)).astype(BF16)
        carry = jnp.zeros((1, LANES), F32)
        for c, (is1, _, _, _) in enumerate(info1):
            carry = jnp.where(n0_1 == c, 0.0, carry)
            key = key_ref[par1, c]
            eq = jnp.where(key == pick(is1, thr1), 1.0, 0.0)
            pre = jnp.dot(tri, eq.astype(BF16), preferred_element_type=F32) + carry
            drop = (eq > 0.0) & (pre > pick(is1, need))
            key_ref[par1, c] = jnp.where(drop, INT_MIN, key)
            carry = carry + jnp.sum(eq, axis=0, keepdims=True)

    @pl.when(jnp.logical_not(den_min > DEN_FLOOR))
    def _():
        acc_ref[...] = jnp.zeros_like(acc_ref)
        m = [jnp.full((1, wide), -jnp.inf, F32)] * 2
        for c, (is1, slot, _, _) in enumerate(info2):
            s = jnp.where(selected(c), logits(c), -jnp.inf)
            m_old = pick(is1, m)
            m_new = jnp.maximum(m_old, jnp.max(s, axis=0, keepdims=True))
            m_safe = jnp.where(m_new == -jnp.inf, 0.0, m_new)
            acc_ref[slot] = acc_ref[slot] * jnp.exp2(m_old - m_safe) + weighted_values(c, jnp.exp2(s - m_safe))
            if isinstance(is1, bool):
                m[int(is1)] = m_new
            else:
                m = [jnp.where(is1, m[0], m_new), jnp.where(is1, m_new, m[1])]

    lo2 = lax.broadcasted_iota(I32, (BLOCK, LANES), 1) < HEAD_DIM
    for slot in range(2):
        acc = acc_ref[slot]
        o_t = acc / acc[HEAD_DIM:HEAD_DIM + 1, :]
        for tp in range(n_heads // 2):
            ev = o_t[:, (2 * tp) * LANES:(2 * tp + 1) * LANES].T
            od = o_t[:, (2 * tp + 1) * LANES:(2 * tp + 2) * LANES].T
            o_ref[0, tiles2[slot], :, tp * LANES:(tp + 1) * LANES] = (
                jnp.where(lo2, ev, pltpu.roll(od, HEAD_DIM, 1)).astype(o_ref.dtype))


def _dsa_pipe(c_pack, wi, batch, seq):
    n = batch * seq
    nq = seq // BLOCK
    assert nq % 2 == 0 and (nq * BLOCK) % DSA_CHUNK == 0
    pairs = nq // 2
    n_steps = batch * pairs + 1
    n_chunks = (nq * BLOCK) // DSA_CHUNK + 1
    k_sel = min(TOPK_MAX, seq // 4)
    qw = C_HEADS * HEAD_DIM
    sel_step = lambda t: jnp.minimum(t, n_steps - 2)
    first = lambda t: (sel_step(t) // pairs) * nq + sel_step(t) % pairs
    second = lambda t: (sel_step(t) // pairs) * nq + nq - 1 - sel_step(t) % pairs
    sel_row = lambda t: sel_step(t) // pairs
    att_row = lambda t: jnp.maximum(t - 1, 0) // pairs
    qrow = lambda blk, which: pl.BlockSpec((BLOCK, qw), lambda t: (which(t), blk))
    wrow = lambda which: pl.BlockSpec((BLOCK, LANES), lambda t: (which(t), 0))
    full = lambda col, which: pl.BlockSpec((seq, LANES), lambda t: (which(t), col // LANES))
    out = pl.pallas_call(
        functools.partial(_dsa_pipe_kernel, k_sel=k_sel, n_tiles=nq, n_steps=n_steps),
        grid=(n_steps,),
        in_specs=[qrow(0, first), qrow(0, second), qrow(1, first), qrow(1, second),
                  wrow(first), wrow(second),
                  full(C_KI, sel_row), full(C_KC, att_row), full(C_VC, att_row)],
        out_specs=pl.BlockSpec((1, nq, BLOCK, qw), lambda t: (att_row(t), 0, 0, 0)),
        out_shape=jax.ShapeDtypeStruct((batch, nq, BLOCK, qw), BF16),
        scratch_shapes=[
            pltpu.VMEM((2, 2, C_HEADS * LANES, 2 * LANES), BF16),
            pltpu.VMEM((2, IDX_HEADS * LANES, LANES), BF16),
            pltpu.VMEM((2, LANES, LANES), F32),
            pltpu.VMEM((2, n_chunks, DSA_CHUNK, LANES), I32),
            pltpu.VMEM((n_chunks, DSA_CHUNK, LANES), I16),
            pltpu.VMEM((n_chunks, DSA_CHUNK, LANES), I16),
            pltpu.VMEM((2, 2, 8, LANES), I32),
            pltpu.VMEM((2, LANES, C_HEADS * LANES), F32),
            pltpu.VMEM((2, DSA_CHUNK, C_HEADS * LANES), F32),
        ],
        compiler_params=_params("arbitrary"),
        name="dsa_pipe",
    )(c_pack, c_pack, c_pack, c_pack, wi, wi, c_pack, c_pack, c_pack)
    return out.reshape(n, qw)


def _merge_kernel(x_ref, oa_ref, ob_ref, oc_ref,
                  wg_ref, wa_ref, wb_ref, wc_ref, wo_ref, g_ref, b_ref, o_ref, *, alpha):
    x = x_ref[...]
    xb = x.astype(BF16)
    d = x.shape[1]

    def branch(o_ref_, w_ref, k):
        gate = jnp.dot(xb, wg_ref[:, k * d:(k + 1) * d], preferred_element_type=F32)
        return jax.nn.sigmoid(gate) * jnp.dot(o_ref_[...], w_ref[...], preferred_element_type=F32)

    merged = branch(oa_ref, wa_ref, 0) + branch(ob_ref, wb_ref, 1) + branch(oc_ref, wc_ref, 2)
    out = jnp.dot(merged.astype(BF16), wo_ref[...], preferred_element_type=F32)
    o_ref[...] = _layer_norm(alpha * x + out, g_ref[...], b_ref[...])


def _merge(x, oa, ob, oc, w_gates, w_a, w_b, w_c, w_o, g, b, alpha):
    n, d = x.shape
    tm = 512
    bw = oa.shape[1]
    row = lambda w: pl.BlockSpec((tm, w), lambda i: (i, 0))
    const = lambda a: pl.BlockSpec(a.shape, lambda i: (0, 0))
    return pl.pallas_call(
        functools.partial(_merge_kernel, alpha=alpha),
        grid=(n // tm,),
        in_specs=[row(d)] + [row(bw)] * 3
        + [const(w_gates), const(w_a), const(w_b), const(w_c), const(w_o), const(g), const(b)],
        out_specs=row(d),
        out_shape=jax.ShapeDtypeStruct((n, d), F32),
        compiler_params=_params("parallel"),
        name="merge_ln",
    )(x, oa, ob, oc, w_gates, w_a, w_b, w_c, w_o, g, b)


def _mixer(x, cos, sin, w_in, sink_b, w_br_a, w_br_b, w_br_c, w_out, g, b, alpha, batch, seq):
    n = batch * seq
    w_packed, w_gates = _pack_in_weights(w_in)
    a_pack, b_pack, c_pack, wi = _inproj(x, w_packed, cos, sin)

    oa = _band_attention(a_pack, batch=batch, seq=seq, q_col=0, k_col=A_W, v_col=2 * A_W,
                         kv_tile_of=lambda t: t,
                         configs=tuple((dil, window // dil) for window, dil in A_CONFIGS))
    ob = _band_attention(b_pack, batch=batch, seq=seq, q_col=0, k_col=512, v_col=768,
                         kv_tile_of=lambda t: t // 2, configs=((1, B_WINDOW - 1),), sink=sink_b)
    oc = _dsa_pipe(c_pack, wi, batch, seq)

    return _merge(x, oa, ob, oc, w_gates, w_br_a.astype(BF16), w_br_b.astype(BF16),
                  w_br_c.astype(BF16), w_out.astype(BF16), g, b, alpha)


def kernel(x, positions, w_in, sink_b, w_br_a, w_br_b, w_br_c, w_out,
           ffn1_in, ffn1_out, ffn2_in, ffn2_out, ln_g, ln_b):
    batch, seq, d = x.shape
    depth = w_in.shape[0]
    alpha = float((2 * depth) ** 0.25)
    cos, sin = _rope_tables(positions)
    h = x.reshape(batch * seq, d)
    for l in range(depth):
        gb = lambda k: (ln_g[l, k][None, :], ln_b[l, k][None, :])
        h = _ffn(h, ffn1_in[l].astype(BF16), ffn1_out[l].astype(BF16), *gb(0), alpha)
        h = _mixer(h, cos, sin, w_in[l], sink_b[l], w_br_a[l], w_br_b[l], w_br_c[l], w_out[l],
                   *gb(1), alpha, batch, seq)
        h = _ffn(h, ffn2_in[l].astype(BF16), ffn2_out[l].astype(BF16), *gb(2), alpha)
    return h.reshape(batch, seq, d)
```

```python
import functools

import jax
import jax.numpy as jnp
import numpy as np
from jax import lax
from jax.experimental import pallas as pl
from jax.experimental.pallas import tpu as pltpu

HEAD_DIM = 64
ROPE_THETA = 10000.0
A_HEADS = 8
A_CONFIGS = ((128, 1), (512, 4), (2048, 16))
B_HEADS = 8
B_KV_HEADS = 2
B_WINDOW = 128
C_HEADS = 8
IDX_HEADS = 8
TOPK_MAX = 256
BLOCK = 128
LN_EPS = 1e-5
LANES = 128
VMEM_LIMIT_BYTES = 56 * 1024 * 1024
INT_MIN = -(2 ** 31)
I16_MIN = -(2 ** 15)
LOG2E = 1.4426950408889634

F32 = jnp.float32
BF16 = jnp.bfloat16
I16 = jnp.int16
I32 = jnp.int32


def _params(*sem):
    return pltpu.CompilerParams(dimension_semantics=sem, vmem_limit_bytes=VMEM_LIMIT_BYTES)


def _layer_norm(y, g, b):
    mu = jnp.mean(y, axis=-1, keepdims=True)
    d = y - mu
    var = jnp.mean(d * d, axis=-1, keepdims=True)
    return d * lax.rsqrt(var + LN_EPS) * g + b


def _rope_table_kernel(pos_ref, inv_ref, sign_ref, cos_ref, sin_ref):
    ang = pos_ref[...].astype(F32) * inv_ref[...]
    cos_ref[...] = jnp.cos(ang)
    sin_ref[...] = jnp.sin(ang) * sign_ref[...]


def _rope_tables(positions):
    n = positions.size
    tm = 1024
    inv = ROPE_THETA ** (-jnp.arange(0, HEAD_DIM, 2, dtype=F32) / HEAD_DIM)
    inv = jnp.tile(inv, LANES // inv.shape[0])[None, :]
    half = (np.arange(LANES) % HEAD_DIM) < HEAD_DIM // 2
    sign = jnp.asarray(np.where(half, -1.0, 1.0), F32)[None, :]
    row = pl.BlockSpec((tm, LANES), lambda i: (i, 0))
    const = pl.BlockSpec((1, LANES), lambda i: (0, 0))
    return pl.pallas_call(
        _rope_table_kernel,
        grid=(n // tm,),
        in_specs=[pl.BlockSpec((tm, 1), lambda i: (i, 0)), const, const],
        out_specs=[row, row],
        out_shape=[jax.ShapeDtypeStruct((n, LANES), F32)] * 2,
        compiler_params=_params("parallel"),
        name="rope_tables",
    )(positions.reshape(n, 1), inv, sign)


FFN_SUB = 256


def _ffn_kernel(x_ref, wg_ref, wu_ref, wo_ref, g_ref, b_ref, o_ref, *, alpha):
    for r0 in range(0, x_ref.shape[0], FFN_SUB):
        x = x_ref[r0:r0 + FFN_SUB, :]
        xb = x.astype(BF16)
        gate = jnp.dot(xb, wg_ref[...], preferred_element_type=F32)
        up = jnp.dot(xb, wu_ref[...], preferred_element_type=F32)
        h = (gate * jax.nn.sigmoid(gate) * up).astype(BF16)
        y = alpha * x + 0.5 * jnp.dot(h, wo_ref[...], preferred_element_type=F32)
        o_ref[r0:r0 + FFN_SUB, :] = _layer_norm(y, g_ref[...], b_ref[...])


def _resident(shape, index_map):
    return pl.BlockSpec(shape, index_map, pipeline_mode=pl.Buffered(1))


def _ffn(x, w_in, w_out, g, b, alpha):
    n, d = x.shape
    d_ff = w_out.shape[0]
    tm = 512
    return pl.pallas_call(
        functools.partial(_ffn_kernel, alpha=alpha),
        grid=(n // tm,),
        in_specs=[
            pl.BlockSpec((tm, d), lambda i: (i, 0)),
            _resident((d, d_ff), lambda i: (0, 0)),
            _resident((d, d_ff), lambda i: (0, 1)),
            _resident((d_ff, d), lambda i: (0, 0)),
            _resident((1, d), lambda i: (0, 0)),
            _resident((1, d), lambda i: (0, 0)),
        ],
        out_specs=pl.BlockSpec((tm, d), lambda i: (i, 0)),
        out_shape=jax.ShapeDtypeStruct((n, d), F32),
        compiler_params=_params("parallel"),
        name="ffn_ln",
    )(x, w_in, w_in, w_out, g, b)


A_W = A_HEADS * HEAD_DIM
PACK_A = 3 * A_W
PACK_B = B_HEADS * HEAD_DIM + 4 * LANES
PACK_C = 2 * C_HEADS * HEAD_DIM + 3 * LANES
C_KC, C_KI, C_VC = 1024, 1152, 1280


def _pack_in_weights(w):
    sizes = (A_W, A_W, A_W, 512, 128, 128, 512, 64, 64, 512, 64, 8, 1024, 1024, 1024)
    offs = np.concatenate([[0], np.cumsum(sizes)])
    (qa, ka, va, qb, kb, vb, qc, kc, vc, qi, ki, wi, ga, gb, gc) = [
        w[:, offs[i]:offs[i + 1]] for i in range(len(sizes))]

    def dup_heads(t):
        g = t.shape[1] // HEAD_DIM
        t = t.reshape(t.shape[0], g, 1, HEAD_DIM)
        return jnp.broadcast_to(t, (t.shape[0], g, 2, HEAD_DIM)).reshape(t.shape[0], g * LANES)

    wi_pad = jnp.pad(wi, ((0, 0), (0, LANES - wi.shape[1])))
    packed = jnp.concatenate(
        [qa, ka, va, qb, dup_heads(kb), dup_heads(vb),
         qc, qi, dup_heads(kc), dup_heads(ki), dup_heads(vc), wi_pad], axis=1)
    gates = jnp.concatenate([ga, gb, gc], axis=1)
    return packed.astype(BF16), gates.astype(BF16)


def _rope(h, cos, sin):
    w = h.shape[1]
    reps = w // LANES
    c = jnp.tile(cos, (1, reps)) if reps > 1 else cos
    s = jnp.tile(sin, (1, reps)) if reps > 1 else sin
    lane = lax.broadcasted_iota(jnp.int32, h.shape, 1)
    first = (lane % HEAD_DIM) < HEAD_DIM // 2
    partner = jnp.where(first, pltpu.roll(h, w - HEAD_DIM // 2, 1), pltpu.roll(h, HEAD_DIM // 2, 1))
    return h * c + partner * s


def _inproj_kernel(x_ref, w_ref, cos_ref, sin_ref, a_ref, b_ref, c_ref, wi_ref):
    xb = x_ref[...].astype(BF16)
    cos = cos_ref[...]
    sin = sin_ref[...]
    scale = HEAD_DIM ** -0.5

    def proj(c0, width):
        return jnp.dot(xb, w_ref[:, c0:c0 + width], preferred_element_type=F32)

    a_ref[:, 0:512] = _rope(proj(0, 512), cos, sin) * (scale * LOG2E)
    a_ref[:, 512:1024] = _rope(proj(512, 512), cos, sin)
    a_ref[:, 1024:1536] = proj(1024, 512)
    o = PACK_A
    b_ref[:, 0:512] = (_rope(proj(o, 512), cos, sin) * (scale * LOG2E)).astype(BF16)
    b_ref[:, 512:768] = _rope(proj(o + 512, 256), cos, sin).astype(BF16)
    b_ref[:, 768:1024] = proj(o + 768, 256).astype(BF16)
    o = PACK_A + PACK_B
    c_ref[:, 0:512] = (_rope(proj(o, 512), cos, sin) * (scale * LOG2E)).astype(BF16)
    c_ref[:, 512:1024] = _rope(proj(o + 512, 512), cos, sin).astype(BF16)
    c_ref[:, 1024:1280] = _rope(proj(o + 1024, 256), cos, sin).astype(BF16)
    lane = lax.broadcasted_iota(I32, (xb.shape[0], LANES), 1)
    c_ref[:, 1280:1408] = jnp.where(lane < HEAD_DIM, proj(o + 1280, 128), 1.0).astype(BF16)
    wi_ref[...] = proj(o + 1408, 128)


def _inproj(x, w_packed, cos, sin):
    n, d = x.shape
    tm = 512
    wtot = w_packed.shape[1]
    row = lambda w: pl.BlockSpec((tm, w), lambda i: (i, 0))
    return pl.pallas_call(
        _inproj_kernel,
        grid=(n // tm,),
        in_specs=[row(d), pl.BlockSpec((d, wtot), lambda i: (0, 0)), row(LANES), row(LANES)],
        out_specs=[row(PACK_A), row(PACK_B), row(PACK_C), row(LANES)],
        out_shape=[
            jax.ShapeDtypeStruct((n, PACK_A), F32),
            jax.ShapeDtypeStruct((n, PACK_B), BF16),
            jax.ShapeDtypeStruct((n, PACK_C), BF16),
            jax.ShapeDtypeStruct((n, LANES), F32),
        ],
        compiler_params=_params("parallel"),
        name="inproj_rope",
    )(x, w_packed, cos, sin)


def _band_kernel(*refs, configs, seq, has_sink):
    q_ref, k_ref, v_ref = refs[:3]
    rest = refs[3:]
    if has_sink:
        sink_ref, rest = rest[0], rest[1:]
    o_ref, qs_ref, kf_ref, vf_ref = rest[:4]
    multi = len(configs) > 1
    if multi:
        num_ref, lse_ref = rest[4:6]

    lane = lax.broadcasted_iota(I32, (1, LANES), 1)
    lo = lane < HEAD_DIM
    if has_sink:
        sk = jnp.max(jnp.concatenate([jnp.broadcast_to(sink_ref[0, 0:1, :], (BLOCK, LANES)),
                                      jnp.broadcast_to(sink_ref[0, 1:2, :], (BLOCK, LANES))], axis=0),
                     axis=-1, keepdims=True) * LOG2E

    def valid_mask(nk, max_dist):
        qi = lax.broadcasted_iota(I32, (BLOCK, nk), 0)
        kj = lax.broadcasted_iota(I32, (BLOCK, nk), 1)
        dist = qi + (nk - BLOCK) - kj
        ok = (dist >= 0) & (dist <= max_dist)
        return jnp.concatenate([ok, ok], axis=0)

    def rows(ref, start, size, stride):
        if stride == 1:
            return ref[0, start:start + size, :]
        return ref[0, pl.ds(start, size, stride=stride), :]

    for c, (dil, max_dist) in enumerate(configs):
        fl = seq // dil
        for r in range(dil):
            dst = slice(r * fl, (r + 1) * fl)
            x = rows(q_ref, r, fl, dil).astype(F32)
            qs_ref[0, dst, :] = jnp.where(lo, x, 0.0).astype(BF16)
            qs_ref[1, dst, :] = jnp.where(lo, 0.0, x).astype(BF16)
            kf_ref[dst, :] = rows(k_ref, r, fl, dil).astype(BF16)
            vf_ref[dst, :] = rows(v_ref, r, fl, dil).astype(BF16)
        valid_full = valid_mask(2 * BLOCK, max_dist)
        valid_first = valid_mask(BLOCK, max_dist)
        for r in range(dil):
            for jb in range(fl // BLOCK):
                r0 = r * fl + jb * BLOCK
                k0 = r0 - BLOCK if jb > 0 else r0
                valid = valid_full if jb > 0 else valid_first
                q2 = jnp.concatenate([qs_ref[0, r0:r0 + BLOCK, :], qs_ref[1, r0:r0 + BLOCK, :]], axis=0)
                s = lax.dot_general(q2, kf_ref[k0:r0 + BLOCK, :], (((1,), (1,)), ((), ())),
                                    preferred_element_type=F32)
                s = jnp.where(valid, s, -jnp.inf)
                m = jnp.max(s, axis=-1, keepdims=True)
                if has_sink:
                    m = jnp.maximum(m, sk)
                p = jnp.exp2(s - m)
                den = jnp.sum(p, axis=-1, keepdims=True)
                if has_sink:
                    den = den + jnp.exp2(sk - m)
                pv = jnp.dot(p.astype(BF16), vf_ref[k0:r0 + BLOCK, :], preferred_element_type=F32)
                num = jnp.where(lo, pv[:BLOCK], pv[BLOCK:])
                den2 = jnp.where(lo, den[:BLOCK], den[BLOCK:])
                t0 = r + dil * BLOCK * jb
                if not multi:
                    o_ref[0, t0:t0 + BLOCK, :] = (num / den2).astype(o_ref.dtype)
                    continue
                lse2 = jnp.where(lo, m[:BLOCK], m[BLOCK:]) + jnp.log2(den2)
                for ref, val in ((num_ref, num / den2), (lse_ref, lse2)):
                    if dil == 1:
                        ref[c, t0:t0 + BLOCK, :] = val
                    else:
                        ref[c, pl.ds(t0, BLOCK, stride=dil), :] = val

    if multi:
        step = 2 * BLOCK
        for t0 in range(0, seq, step):
            sl = slice(t0, t0 + step)
            ls = [lse_ref[c, sl, :] for c in range(len(configs))]
            top = functools.reduce(jnp.maximum, ls)
            ws = [jnp.exp2(lc - top) for lc in ls]
            numer = sum(w * num_ref[c, sl, :] for c, w in enumerate(ws))
            o_ref[0, sl, :] = (numer / sum(ws)).astype(o_ref.dtype)


def _band_attention(qkv, *, batch, seq, q_col, k_col, v_col, kv_tile_of, configs, sink=None):
    n_tiles = 4
    qkv = qkv.reshape(batch, seq, qkv.shape[-1])
    tile = lambda col_of: pl.BlockSpec((1, seq, LANES), lambda b, t: (b, 0, col_of(t)))
    in_specs = [tile(lambda t: q_col // LANES + t),
                tile(lambda t: k_col // LANES + kv_tile_of(t)),
                tile(lambda t: v_col // LANES + kv_tile_of(t))]
    args = [qkv] * 3
    if sink is not None:
        sink_tiles = jnp.broadcast_to(sink.astype(F32).reshape(n_tiles, 2, 1), (n_tiles, 2, LANES))
        in_specs.append(pl.BlockSpec((1, 2, LANES), lambda b, t: (t, 0, 0)))
        args.append(sink_tiles)
    scratch = [pltpu.VMEM((2, seq, LANES), BF16), pltpu.VMEM((seq, LANES), BF16),
               pltpu.VMEM((seq, LANES), BF16)]
    if len(configs) > 1:
        scratch += [pltpu.VMEM((len(configs), seq, LANES), F32)] * 2
    out = pl.pallas_call(
        functools.partial(_band_kernel, configs=configs, seq=seq, has_sink=sink is not None),
        grid=(batch, n_tiles),
        in_specs=in_specs,
        out_specs=pl.BlockSpec((1, seq, LANES), lambda b, t: (b, 0, t)),
        out_shape=jax.ShapeDtypeStruct((batch, seq, n_tiles * LANES), BF16),
        scratch_shapes=scratch,
        compiler_params=_params("parallel", "arbitrary"),
        name="band_attention",
    )(*args)
    return out.reshape(batch * seq, n_tiles * LANES)


DSA_CHUNK = 2 * BLOCK
BOUND_SLACK = 1.03
DEN_FLOOR = 2.0 ** -60
NEG_BIG = -(2.0 ** 100)


def _fold_rows(x, rows):
    parts = [x[r:r + rows] for r in range(0, x.shape[0], rows)]
    while len(parts) > 1:
        parts = [a + b for a, b in zip(parts[::2], parts[1::2])] + (parts[-1:] if len(parts) % 2 else [])
    return parts[0]


def _dsa_pipe_kernel(qc0_ref, qc1_ref, qi0_ref, qi1_ref, wi0_ref, wi1_ref, ki_ref, kc_ref, vc_ref, o_ref,
                     qsc_ref, qsi_ref, wt_ref, key_ref, hi_ref, lo_ref, thr_ref, acc_ref, s_ref,
                     *, k_sel, n_tiles, n_steps):
    t = pl.program_id(0)
    n_heads = C_HEADS
    ch = DSA_CHUNK
    pairs = n_tiles // 2
    n_chunks = (n_tiles * BLOCK) // ch + 1
    max_n0 = (pairs * BLOCK + ch - 1) // ch
    wide = n_heads * LANES
    par1 = t % 2
    par2 = 1 - par1
    j1 = jnp.minimum(t, n_steps - 2) % pairs
    j2 = jnp.maximum(t - 1, 0) % pairs

    @pl.when(t == 0)
    def _():
        qsc_ref[1] = jnp.zeros(qsc_ref.shape[1:], BF16)
        eye = (lax.broadcasted_iota(I32, (LANES, LANES), 0)
               == lax.broadcasted_iota(I32, (LANES, LANES), 1)).astype(F32).astype(BF16)
        for par in range(2):
            for slot in range(2):
                for h in range(n_heads):
                    qsc_ref[par, slot, h * LANES:(h + 1) * LANES, LANES:2 * LANES] = eye
        key_ref[1] = jnp.zeros(key_ref.shape[1:], I32)
        thr_ref[1] = jnp.zeros(thr_ref.shape[1:], I32)

    def chunk_infos(j):
        n0 = (j * BLOCK + BLOCK + ch - 1) // ch
        tiles = (j, n_tiles - 1 - j)
        out = []
        for c in range(n_chunks):
            if c == 0:
                is1, slot, base = False, 0, 0
            elif c >= max_n0:
                is1, slot, base = True, 1, n0
            else:
                is1 = c >= n0
                slot, base = jnp.where(is1, 1, 0), jnp.where(is1, n0, 0)
            k0 = pl.multiple_of((c - base) * ch, ch)
            q0 = (tiles[slot] if isinstance(is1, bool) else jnp.where(is1, tiles[1], tiles[0])) * BLOCK
            out.append((is1, slot, k0, q0))
        return n0, tiles, out

    def pick(is1, a):
        return a[int(is1)] if isinstance(is1, bool) else jnp.where(is1, a[1], a[0])

    def add_to_slot(acc, is1, part):
        if isinstance(is1, bool):
            acc[int(is1)] = acc[int(is1)] + part
        else:
            zero = jnp.zeros_like(part)
            acc[0] = acc[0] + jnp.where(is1, zero, part)
            acc[1] = acc[1] + jnp.where(is1, part, zero)

    lane = lax.broadcasted_iota(I32, (1, LANES), 1)
    head_mask = [jnp.where(lane < HEAD_DIM, 1.0, 0.0).astype(BF16),
                 jnp.where(lane < HEAD_DIM, 0.0, 1.0).astype(BF16)]

    n0_2, tiles2, info2 = chunk_infos(j2)
    thr2 = [thr_ref[par2, s, 0:1, :] for s in range(2)]

    def logits(c):
        _, slot, k0, _ = info2[c]
        return lax.dot_general(kc_ref[pl.ds(k0, ch), :], qsc_ref[par2, slot, :, 0:LANES],
                               (((1,), (1,)), ((), ())), preferred_element_type=F32)

    def selected(c):
        return jnp.tile(key_ref[par2, c] >= pick(info2[c][0], thr2), (1, n_heads))

    def weighted_values(c, p):
        return lax.dot_general(vc_ref[pl.ds(info2[c][2], ch), :], p.astype(BF16),
                               (((0,), (0,)), ((), ())), preferred_element_type=F32)

    ksq = kc_ref[...].astype(F32)
    ksq = 0.5 * jnp.max(jnp.sum(ksq * ksq, axis=1, keepdims=True), axis=0, keepdims=True)
    ones8 = jnp.ones((8, LANES), BF16)
    bound = []
    for slot in range(2):
        q2 = qsc_ref[par2, slot, :, 0:LANES].astype(F32)
        qsq = lax.dot_general(ones8, (q2 * q2).astype(BF16), (((1,), (1,)), ((), ())),
                              preferred_element_type=F32)[0:1, :]
        per_head = jnp.sqrt(qsq * ksq) * BOUND_SLACK
        bound.append(functools.reduce(
            jnp.maximum, [per_head[:, h * LANES:(h + 1) * LANES] for h in range(n_heads)]))
    acc_ref[...] = jnp.zeros_like(acc_ref)

    def chunk_place(c):
        is1 = c >= n0_2
        return is1, jnp.where(is1, 1, 0), pl.multiple_of((c - jnp.where(is1, n0_2, 0)) * ch, ch)

    def logits_stage(c):
        is1, slot, k0 = chunk_place(c)
        sel = key_ref[par2, c] >= jnp.where(is1, thr2[1], thr2[0])
        bias = jnp.where(sel, -jnp.where(is1, bound[1], bound[0]), NEG_BIG).astype(BF16)
        lhs = jnp.concatenate([kc_ref[pl.ds(k0, ch), :], bias], axis=1)
        s_ref[c % 2] = lax.dot_general(lhs, qsc_ref[par2, slot], (((1,), (1,)), ((), ())),
                                       preferred_element_type=F32)

    def values_stage(c):
        _, slot, k0 = chunk_place(c)
        acc_ref[slot] += lax.dot_general(vc_ref[pl.ds(k0, ch), :], jnp.exp2(s_ref[c % 2]).astype(BF16),
                                         (((0,), (0,)), ((), ())), preferred_element_type=F32)

    def attention_chunk(c):
        values_stage(c)
        nxt = jnp.minimum(c + 1, n_chunks - 1)
        logits_stage(nxt)
        never = lambda x: jnp.where((pltpu.bitcast(x, I32) | 1) == 0, 1, 0)
        return never(acc_ref[0, 0:1, 0:LANES]), never(s_ref[nxt % 2, 0:1, 0:LANES])

    logits_stage(0)

    n0_1, tiles1, info1 = chunk_infos(j1)
    for slot, (qc_ref, qi_ref, wi_ref) in enumerate(((qc0_ref, qi0_ref, wi0_ref),
                                                     (qc1_ref, qi1_ref, wi1_ref))):
        for h in range(n_heads):
            tl = slice((h // 2) * LANES, (h // 2 + 1) * LANES)
            rows_h = slice(h * LANES, (h + 1) * LANES)
            qsc_ref[par1, slot, rows_h, 0:LANES] = qc_ref[:, tl] * head_mask[h % 2]
            qsi_ref[slot, rows_h, :] = qi_ref[:, tl] * head_mask[h % 2]
        wt_ref[slot] = wi_ref[...].T

    row_minus_col = (lax.broadcasted_iota(I32, (ch, LANES), 0)
                     - lax.broadcasted_iota(I32, (ch, LANES), 1))
    for c, (is1, slot, k0, q0) in enumerate(info1):
        d = lax.dot_general(ki_ref[pl.ds(k0, ch), :], qsi_ref[slot],
                            (((1,), (1,)), ((), ())), preferred_element_type=F32)
        sc = jnp.zeros((ch, LANES), F32)
        for h in range(IDX_HEADS):
            sc = sc + jnp.maximum(d[:, h * LANES:(h + 1) * LANES], 0.0) * wt_ref[slot, h:h + 1, :]
        bits = pltpu.bitcast(sc, I32)
        key = bits ^ ((bits >> 31) & jnp.int32(0x7FFFFFFF))
        key = jnp.where(sc == 0.0, 0, key)
        key = jnp.where(row_minus_col <= q0 - k0, key, INT_MIN)
        key_ref[par1, c] = key
        hi_ref[c] = (key >> 16).astype(I16)
        lo_ref[c] = ((key & 0xFFFF) - 32768).astype(I16)

    def count16(ref, cands, strict=False):
        acc = [jnp.zeros((16, LANES), I16), jnp.zeros((16, LANES), I16)]
        for c, (is1, _, _, _) in enumerate(info1):
            cand = pick(is1, cands)
            blk = ref[c]
            hit = jnp.where(blk > cand if strict else blk >= cand, jnp.int16(1), jnp.int16(0))
            add_to_slot(acc, is1, _fold_rows(hit, 16))
        return [jnp.sum(a.astype(I32), axis=0, keepdims=True) for a in acc]

    bits_per_trip = 4

    def search16(ref, need, first_chunk, trips):
        assert trips * bits_per_trip >= 16

        def trip(i, tu):
            zero_a, zero_b = attention_chunk(first_chunk + i)
            for b in range(bits_per_trip):
                it = i * bits_per_trip + b
                bit = jnp.where(it < 16, lax.shift_left(jnp.int32(1), jnp.maximum(15 - it, 0)), 0)
                cu = [u | bit for u in tu]
                cnt = count16(ref, [(u - 32768).astype(I16) for u in cu])
                tu = tuple(jnp.where(cnt[s] >= need[s], cu[s], tu[s]) for s in range(2))
            return tuple(u | zero_a | zero_b for u in tu)

        z = jnp.zeros((1, LANES), I32)
        tu = lax.fori_loop(0, trips, trip, (z, z))
        return [u - 32768 for u in tu]

    trips_hi = (n_chunks + 1) // 2
    ksel = jnp.full((1, LANES), k_sel, I32)
    p1 = search16(hi_ref, [ksel, ksel], 0, trips_hi)
    p1_16 = [p.astype(I16) for p in p1]
    above = count16(hi_ref, p1_16, strict=True)
    for c, (is1, _, _, _) in enumerate(info1):
        lo_ref[c] = jnp.where(hi_ref[c] == pick(is1, p1_16), lo_ref[c], jnp.int16(I16_MIN))
    p2 = search16(lo_ref, [ksel - above[0], ksel - above[1]], trips_hi, n_chunks - trips_hi)
    thr1 = [jnp.maximum((p1[s] << 16) | ((p2[s] + 32768) & 0xFFFF), INT_MIN + 1) for s in range(2)]
    for s in range(2):
        thr_ref[par1, s] = jnp.broadcast_to(thr1[s], (8, LANES))

    def count32(pred):
        acc = [jnp.zeros((8, LANES), I32), jnp.zeros((8, LANES), I32)]
        for c, (is1, _, _, _) in enumerate(info1):
            hit = jnp.where(pred(key_ref[par1, c], pick(is1, thr1)), 1, 0)
            add_to_slot(acc, is1, jnp.sum(hit.reshape(ch // 8, 8, LANES), axis=0))
        return [jnp.sum(a, axis=0, keepdims=True) for a in acc]

    c_ge = count32(lambda k, th: k >= th)
    over = jnp.max(jnp.maximum(c_ge[0], c_ge[1])) > k_sel
    den_min = jnp.min(jnp.minimum(acc_ref[0, HEAD_DIM:HEAD_DIM + 1, :],
                                  acc_ref[1, HEAD_DIM:HEAD_DIM + 1, :]))

    @pl.when(over)
    def _():
        c_gt = count32(lambda k, th: k > th)
        need = [(k_sel - c).astype(F32) for c in c_gt]
        tri = (lax.broadcasted_iota(I32, (ch, ch), 1)
               <= lax.broadcasted_iota(I32, (ch, ch), 0)).astype(BF16)
        carry = jnp.zeros((1, LANES), F32)
        for c, (is1, _, _, _) in enumerate(info1):
            carry = jnp.where(n0_1 == c, 0.0, carry)
            key = key_ref[par1, c]
            eq = jnp.where(key == pick(is1, thr1), 1.0, 0.0)
            pre = jnp.dot(tri, eq.astype(BF16), preferred_element_type=F32) + carry
            drop = (eq > 0.0) & (pre > pick(is1, need))
            key_ref[par1, c] = jnp.where(drop, INT_MIN, key)
            carry = carry + jnp.sum(eq, axis=0, keepdims=True)

    @pl.when(jnp.logical_not(den_min > DEN_FLOOR))
    def _():
        acc_ref[...] = jnp.zeros_like(acc_ref)
        m = [jnp.full((1, wide), -jnp.inf, F32)] * 2
        for c, (is1, slot, _, _) in enumerate(info2):
            s = jnp.where(selected(c), logits(c), -jnp.inf)
            m_old = pick(is1, m)
            m_new = jnp.maximum(m_old, jnp.max(s, axis=0, keepdims=True))
            m_safe = jnp.where(m_new == -jnp.inf, 0.0, m_new)
            acc_ref[slot] = acc_ref[slot] * jnp.exp2(m_old - m_safe) + weighted_values(c, jnp.exp2(s - m_safe))
            if isinstance(is1, bool):
                m[int(is1)] = m_new
            else:
                m = [jnp.where(is1, m[0], m_new), jnp.where(is1, m_new, m[1])]

    lo2 = lax.broadcasted_iota(I32, (BLOCK, LANES), 1) < HEAD_DIM
    for slot in range(2):
        acc = acc_ref[slot]
        o_t = acc / acc[HEAD_DIM:HEAD_DIM + 1, :]
        for tp in range(n_heads // 2):
            ev = o_t[:, (2 * tp) * LANES:(2 * tp + 1) * LANES].T
            od = o_t[:, (2 * tp + 1) * LANES:(2 * tp + 2) * LANES].T
            o_ref[0, tiles2[slot], :, tp * LANES:(tp + 1) * LANES] = (
                jnp.where(lo2, ev, pltpu.roll(od, HEAD_DIM, 1)).astype(o_ref.dtype))


def _dsa_pipe(c_pack, wi, batch, seq):
    n = batch * seq
    nq = seq // BLOCK
    assert nq % 2 == 0 and (nq * BLOCK) % DSA_CHUNK == 0
    pairs = nq // 2
    n_steps = batch * pairs + 1
    n_chunks = (nq * BLOCK) // DSA_CHUNK + 1
    k_sel = min(TOPK_MAX, seq // 4)
    qw = C_HEADS * HEAD_DIM
    sel_step = lambda t: jnp.minimum(t, n_steps - 2)
    first = lambda t: (sel_step(t) // pairs) * nq + sel_step(t) % pairs
    second = lambda t: (sel_step(t) // pairs) * nq + nq - 1 - sel_step(t) % pairs
    sel_row = lambda t: sel_step(t) // pairs
    att_row = lambda t: jnp.maximum(t - 1, 0) // pairs
    qrow = lambda blk, which: pl.BlockSpec((BLOCK, qw), lambda t: (which(t), blk))
    wrow = lambda which: pl.BlockSpec((BLOCK, LANES), lambda t: (which(t), 0))
    full = lambda col, which: pl.BlockSpec((seq, LANES), lambda t: (which(t), col // LANES))
    out = pl.pallas_call(
        functools.partial(_dsa_pipe_kernel, k_sel=k_sel, n_tiles=nq, n_steps=n_steps),
        grid=(n_steps,),
        in_specs=[qrow(0, first), qrow(0, second), qrow(1, first), qrow(1, second),
                  wrow(first), wrow(second),
                  full(C_KI, sel_row), full(C_KC, att_row), full(C_VC, att_row)],
        out_specs=pl.BlockSpec((1, nq, BLOCK, qw), lambda t: (att_row(t), 0, 0, 0)),
        out_shape=jax.ShapeDtypeStruct((batch, nq, BLOCK, qw), BF16),
        scratch_shapes=[
            pltpu.VMEM((2, 2, C_HEADS * LANES, 2 * LANES), BF16),
            pltpu.VMEM((2, IDX_HEADS * LANES, LANES), BF16),
            pltpu.VMEM((2, LANES, LANES), F32),
            pltpu.VMEM((2, n_chunks, DSA_CHUNK, LANES), I32),
            pltpu.VMEM((n_chunks, DSA_CHUNK, LANES), I16),
            pltpu.VMEM((n_chunks, DSA_CHUNK, LANES), I16),
            pltpu.VMEM((2, 2, 8, LANES), I32),
            pltpu.VMEM((2, LANES, C_HEADS * LANES), F32),
            pltpu.VMEM((2, DSA_CHUNK, C_HEADS * LANES), F32),
        ],
        compiler_params=_params("arbitrary"),
        name="dsa_pipe",
    )(c_pack, c_pack, c_pack, c_pack, wi, wi, c_pack, c_pack, c_pack)
    return out.reshape(n, qw)


def _merge_kernel(x_ref, oa_ref, ob_ref, oc_ref,
                  wg_ref, wa_ref, wb_ref, wc_ref, wo_ref, g_ref, b_ref, o_ref, *, alpha):
    d = x_ref.shape[1]
    for r0 in range(0, x_ref.shape[0], FFN_SUB):
        rows = slice(r0, r0 + FFN_SUB)
        x = x_ref[rows, :]
        xb = x.astype(BF16)

        def branch(o_ref_, w_ref, k):
            gate = jnp.dot(xb, wg_ref[:, k * d:(k + 1) * d], preferred_element_type=F32)
            return jax.nn.sigmoid(gate) * jnp.dot(o_ref_[rows, :], w_ref[...], preferred_element_type=F32)

        merged = branch(oa_ref, wa_ref, 0) + branch(ob_ref, wb_ref, 1) + branch(oc_ref, wc_ref, 2)
        out = jnp.dot(merged.astype(BF16), wo_ref[...], preferred_element_type=F32)
        o_ref[rows, :] = _layer_norm(alpha * x + out, g_ref[...], b_ref[...])


def _merge(x, oa, ob, oc, w_gates, w_a, w_b, w_c, w_o, g, b, alpha):
    n, d = x.shape
    tm = 512
    bw = oa.shape[1]
    row = lambda w: pl.BlockSpec((tm, w), lambda i: (i, 0))
    const = lambda a: pl.BlockSpec(a.shape, lambda i: (0, 0))
    return pl.pallas_call(
        functools.partial(_merge_kernel, alpha=alpha),
        grid=(n // tm,),
        in_specs=[row(d)] + [row(bw)] * 3
        + [const(w_gates), const(w_a), const(w_b), const(w_c), const(w_o), const(g), const(b)],
        out_specs=row(d),
        out_shape=jax.ShapeDtypeStruct((n, d), F32),
        compiler_params=_params("parallel"),
        name="merge_ln",
    )(x, oa, ob, oc, w_gates, w_a, w_b, w_c, w_o, g, b)


def _mixer(x, cos, sin, w_in, sink_b, w_br_a, w_br_b, w_br_c, w_out, g, b, alpha, batch, seq):
    n = batch * seq
    w_packed, w_gates = _pack_in_weights(w_in)
    a_pack, b_pack, c_pack, wi = _inproj(x, w_packed, cos, sin)

    oa = _band_attention(a_pack, batch=batch, seq=seq, q_col=0, k_col=A_W, v_col=2 * A_W,
                         kv_tile_of=lambda t: t,
                         configs=tuple((dil, window // dil) for window, dil in A_CONFIGS))
    ob = _band_attention(b_pack, batch=batch, seq=seq, q_col=0, k_col=512, v_col=768,
                         kv_tile_of=lambda t: t // 2, configs=((1, B_WINDOW - 1),), sink=sink_b)
    oc = _dsa_pipe(c_pack, wi, batch, seq)

    return _merge(x, oa, ob, oc, w_gates, w_br_a.astype(BF16), w_br_b.astype(BF16),
                  w_br_c.astype(BF16), w_out.astype(BF16), g, b, alpha)


def kernel(x, positions, w_in, sink_b, w_br_a, w_br_b, w_br_c, w_out,
           ffn1_in, ffn1_out, ffn2_in, ffn2_out, ln_g, ln_b):
    batch, seq, d = x.shape
    depth = w_in.shape[0]
    alpha = float((2 * depth) ** 0.25)
    cos, sin = _rope_tables(positions)
    h = x.reshape(batch * seq, d)
    for l in range(depth):
        gb = lambda k: (ln_g[l, k][None, :], ln_b[l, k][None, :])
        h = _ffn(h, ffn1_in[l].astype(BF16), ffn1_out[l].astype(BF16), *gb(0), alpha)
        h = _mixer(h, cos, sin, w_in[l], sink_b[l], w_br_a[l], w_br_b[l], w_br_c[l], w_out[l],
                   *gb(1), alpha, batch, seq)
        h = _ffn(h, ffn2_in[l].astype(BF16), ffn2_out[l].astype(BF16), *gb(2), alpha)
    return h.reshape(batch, seq, d)
```

```python
import functools

import jax
import jax.numpy as jnp
import numpy as np
from jax import lax
from jax.experimental import pallas as pl
from jax.experimental.pallas import tpu as pltpu

HEAD_DIM = 64
ROPE_THETA = 10000.0
A_HEADS = 8
A_CONFIGS = ((128, 1), (512, 4), (2048, 16))
B_HEADS = 8
B_KV_HEADS = 2
B_WINDOW = 128
C_HEADS = 8
IDX_HEADS = 8
TOPK_MAX = 256
BLOCK = 128
LN_EPS = 1e-5
LANES = 128
VMEM_LIMIT_BYTES = 56 * 1024 * 1024
INT_MIN = -(2 ** 31)
I16_MIN = -(2 ** 15)
LOG2E = 1.4426950408889634

F32 = jnp.float32
BF16 = jnp.bfloat16
I16 = jnp.int16
I32 = jnp.int32


def _params(*sem):
    return pltpu.CompilerParams(dimension_semantics=sem, vmem_limit_bytes=VMEM_LIMIT_BYTES)


def _layer_norm(y, g, b):
    mu = jnp.mean(y, axis=-1, keepdims=True)
    d = y - mu
    var = jnp.mean(d * d, axis=-1, keepdims=True)
    return d * lax.rsqrt(var + LN_EPS) * g + b


def _rope_table_kernel(pos_ref, inv_ref, sign_ref, cos_ref, sin_ref):
    ang = pos_ref[...].astype(F32) * inv_ref[...]
    cos_ref[...] = jnp.cos(ang)
    sin_ref[...] = jnp.sin(ang) * sign_ref[...]


def _rope_tables(positions):
    n = positions.size
    tm = 1024
    inv = ROPE_THETA ** (-jnp.arange(0, HEAD_DIM, 2, dtype=F32) / HEAD_DIM)
    inv = jnp.tile(inv, LANES // inv.shape[0])[None, :]
    half = (np.arange(LANES) % HEAD_DIM) < HEAD_DIM // 2
    sign = jnp.asarray(np.where(half, -1.0, 1.0), F32)[None, :]
    row = pl.BlockSpec((tm, LANES), lambda i: (i, 0))
    const = pl.BlockSpec((1, LANES), lambda i: (0, 0))
    return pl.pallas_call(
        _rope_table_kernel,
        grid=(n // tm,),
        in_specs=[pl.BlockSpec((tm, 1), lambda i: (i, 0)), const, const],
        out_specs=[row, row],
        out_shape=[jax.ShapeDtypeStruct((n, LANES), F32)] * 2,
        compiler_params=_params("parallel"),
        name="rope_tables",
    )(positions.reshape(n, 1), inv, sign)


FFN_SUB = 256


def _ffn_kernel(x_ref, wg_ref, wu_ref, wo_ref, g_ref, b_ref, o_ref, *, alpha):
    for r0 in range(0, x_ref.shape[0], FFN_SUB):
        x = x_ref[r0:r0 + FFN_SUB, :]
        xb = x.astype(BF16)
        gate = jnp.dot(xb, wg_ref[...], preferred_element_type=F32)
        up = jnp.dot(xb, wu_ref[...], preferred_element_type=F32)
        h = (gate * jax.nn.sigmoid(gate) * up).astype(BF16)
        y = alpha * x + 0.5 * jnp.dot(h, wo_ref[...], preferred_element_type=F32)
        o_ref[r0:r0 + FFN_SUB, :] = _layer_norm(y, g_ref[...], b_ref[...])


def _resident(shape, index_map):
    return pl.BlockSpec(shape, index_map, pipeline_mode=pl.Buffered(1))


def _ffn(x, w_in, w_out, g, b, alpha):
    n, d = x.shape
    d_ff = w_out.shape[0]
    tm = 1024
    return pl.pallas_call(
        functools.partial(_ffn_kernel, alpha=alpha),
        grid=(n // tm,),
        in_specs=[
            pl.BlockSpec((tm, d), lambda i: (i, 0)),
            _resident((d, d_ff), lambda i: (0, 0)),
            _resident((d, d_ff), lambda i: (0, 1)),
            _resident((d_ff, d), lambda i: (0, 0)),
            _resident((1, d), lambda i: (0, 0)),
            _resident((1, d), lambda i: (0, 0)),
        ],
        out_specs=pl.BlockSpec((tm, d), lambda i: (i, 0)),
        out_shape=jax.ShapeDtypeStruct((n, d), F32),
        compiler_params=_params("parallel"),
        name="ffn_ln",
    )(x, w_in, w_in, w_out, g, b)


A_W = A_HEADS * HEAD_DIM
PACK_A = 3 * A_W
PACK_B = B_HEADS * HEAD_DIM + 4 * LANES
PACK_C = 2 * C_HEADS * HEAD_DIM + 3 * LANES
C_KC, C_KI, C_VC = 1024, 1152, 1280


def _pack_in_weights(w):
    sizes = (A_W, A_W, A_W, 512, 128, 128, 512, 64, 64, 512, 64, 8, 1024, 1024, 1024)
    offs = np.concatenate([[0], np.cumsum(sizes)])
    (qa, ka, va, qb, kb, vb, qc, kc, vc, qi, ki, wi, ga, gb, gc) = [
        w[:, offs[i]:offs[i + 1]] for i in range(len(sizes))]

    def dup_heads(t):
        g = t.shape[1] // HEAD_DIM
        t = t.reshape(t.shape[0], g, 1, HEAD_DIM)
        return jnp.broadcast_to(t, (t.shape[0], g, 2, HEAD_DIM)).reshape(t.shape[0], g * LANES)

    wi_pad = jnp.pad(wi, ((0, 0), (0, LANES - wi.shape[1])))
    packed = jnp.concatenate(
        [qa, ka, va, qb, dup_heads(kb), dup_heads(vb),
         qc, qi, dup_heads(kc), dup_heads(ki), dup_heads(vc), wi_pad], axis=1)
    gates = jnp.concatenate([ga, gb, gc], axis=1)
    return packed.astype(BF16), gates.astype(BF16)


def _rope(h, cos, sin):
    w = h.shape[1]
    reps = w // LANES
    c = jnp.tile(cos, (1, reps)) if reps > 1 else cos
    s = jnp.tile(sin, (1, reps)) if reps > 1 else sin
    lane = lax.broadcasted_iota(jnp.int32, h.shape, 1)
    first = (lane % HEAD_DIM) < HEAD_DIM // 2
    partner = jnp.where(first, pltpu.roll(h, w - HEAD_DIM // 2, 1), pltpu.roll(h, HEAD_DIM // 2, 1))
    return h * c + partner * s


def _inproj_kernel(x_ref, w_ref, cos_ref, sin_ref, a_ref, b_ref, c_ref, wi_ref):
    xb = x_ref[...].astype(BF16)
    cos = cos_ref[...]
    sin = sin_ref[...]
    scale = HEAD_DIM ** -0.5

    def proj(c0, width):
        return jnp.dot(xb, w_ref[:, c0:c0 + width], preferred_element_type=F32)

    a_ref[:, 0:512] = _rope(proj(0, 512), cos, sin) * (scale * LOG2E)
    a_ref[:, 512:1024] = _rope(proj(512, 512), cos, sin)
    a_ref[:, 1024:1536] = proj(1024, 512)
    o = PACK_A
    b_ref[:, 0:512] = (_rope(proj(o, 512), cos, sin) * (scale * LOG2E)).astype(BF16)
    b_ref[:, 512:768] = _rope(proj(o + 512, 256), cos, sin).astype(BF16)
    b_ref[:, 768:1024] = proj(o + 768, 256).astype(BF16)
    o = PACK_A + PACK_B
    c_ref[:, 0:512] = (_rope(proj(o, 512), cos, sin) * (scale * LOG2E)).astype(BF16)
    c_ref[:, 512:1024] = _rope(proj(o + 512, 512), cos, sin).astype(BF16)
    c_ref[:, 1024:1280] = _rope(proj(o + 1024, 256), cos, sin).astype(BF16)
    lane = lax.broadcasted_iota(I32, (xb.shape[0], LANES), 1)
    c_ref[:, 1280:1408] = jnp.where(lane < HEAD_DIM, proj(o + 1280, 128), 1.0).astype(BF16)
    wi_ref[...] = proj(o + 1408, 128)


def _inproj(x, w_packed, cos, sin):
    n, d = x.shape
    tm = 512
    wtot = w_packed.shape[1]
    row = lambda w: pl.BlockSpec((tm, w), lambda i: (i, 0))
    return pl.pallas_call(
        _inproj_kernel,
        grid=(n // tm,),
        in_specs=[row(d), pl.BlockSpec((d, wtot), lambda i: (0, 0)), row(LANES), row(LANES)],
        out_specs=[row(PACK_A), row(PACK_B), row(PACK_C), row(LANES)],
        out_shape=[
            jax.ShapeDtypeStruct((n, PACK_A), F32),
            jax.ShapeDtypeStruct((n, PACK_B), BF16),
            jax.ShapeDtypeStruct((n, PACK_C), BF16),
            jax.ShapeDtypeStruct((n, LANES), F32),
        ],
        compiler_params=_params("parallel"),
        name="inproj_rope",
    )(x, w_packed, cos, sin)


def _band_kernel(*refs, configs, seq, has_sink):
    q_ref, k_ref, v_ref = refs[:3]
    rest = refs[3:]
    if has_sink:
        sink_ref, rest = rest[0], rest[1:]
    o_ref, qs_ref, kf_ref, vf_ref = rest[:4]
    multi = len(configs) > 1
    if multi:
        num_ref, lse_ref = rest[4:6]

    lane = lax.broadcasted_iota(I32, (1, LANES), 1)
    lo = lane < HEAD_DIM
    if has_sink:
        sk = jnp.max(jnp.concatenate([jnp.broadcast_to(sink_ref[0, 0:1, :], (BLOCK, LANES)),
                                      jnp.broadcast_to(sink_ref[0, 1:2, :], (BLOCK, LANES))], axis=0),
                     axis=-1, keepdims=True) * LOG2E

    def valid_mask(nk, max_dist):
        qi = lax.broadcasted_iota(I32, (BLOCK, nk), 0)
        kj = lax.broadcasted_iota(I32, (BLOCK, nk), 1)
        dist = qi + (nk - BLOCK) - kj
        ok = (dist >= 0) & (dist <= max_dist)
        return jnp.concatenate([ok, ok], axis=0)

    def rows(ref, start, size, stride):
        if stride == 1:
            return ref[0, start:start + size, :]
        return ref[0, pl.ds(start, size, stride=stride), :]

    for c, (dil, max_dist) in enumerate(configs):
        fl = seq // dil
        for r in range(dil):
            dst = slice(r * fl, (r + 1) * fl)
            x = rows(q_ref, r, fl, dil).astype(F32)
            qs_ref[0, dst, :] = jnp.where(lo, x, 0.0).astype(BF16)
            qs_ref[1, dst, :] = jnp.where(lo, 0.0, x).astype(BF16)
            kf_ref[dst, :] = rows(k_ref, r, fl, dil).astype(BF16)
            vf_ref[dst, :] = rows(v_ref, r, fl, dil).astype(BF16)
        valid_full = valid_mask(2 * BLOCK, max_dist)
        valid_first = valid_mask(BLOCK, max_dist)
        for r in range(dil):
            for jb in range(fl // BLOCK):
                r0 = r * fl + jb * BLOCK
                k0 = r0 - BLOCK if jb > 0 else r0
                valid = valid_full if jb > 0 else valid_first
                q2 = jnp.concatenate([qs_ref[0, r0:r0 + BLOCK, :], qs_ref[1, r0:r0 + BLOCK, :]], axis=0)
                s = lax.dot_general(q2, kf_ref[k0:r0 + BLOCK, :], (((1,), (1,)), ((), ())),
                                    preferred_element_type=F32)
                s = jnp.where(valid, s, -jnp.inf)
                m = jnp.max(s, axis=-1, keepdims=True)
                if has_sink:
                    m = jnp.maximum(m, sk)
                p = jnp.exp2(s - m)
                den = jnp.sum(p, axis=-1, keepdims=True)
                if has_sink:
                    den = den + jnp.exp2(sk - m)
                pv = jnp.dot(p.astype(BF16), vf_ref[k0:r0 + BLOCK, :], preferred_element_type=F32)
                num = jnp.where(lo, pv[:BLOCK], pv[BLOCK:])
                den2 = jnp.where(lo, den[:BLOCK], den[BLOCK:])
                t0 = r + dil * BLOCK * jb
                if not multi:
                    o_ref[0, t0:t0 + BLOCK, :] = (num / den2).astype(o_ref.dtype)
                    continue
                lse2 = jnp.where(lo, m[:BLOCK], m[BLOCK:]) + jnp.log2(den2)
                for ref, val in ((num_ref, num / den2), (lse_ref, lse2)):
                    if dil == 1:
                        ref[c, t0:t0 + BLOCK, :] = val
                    else:
                        ref[c, pl.ds(t0, BLOCK, stride=dil), :] = val

    if multi:
        step = 2 * BLOCK
        for t0 in range(0, seq, step):
            sl = slice(t0, t0 + step)
            ls = [lse_ref[c, sl, :] for c in range(len(configs))]
            top = functools.reduce(jnp.maximum, ls)
            ws = [jnp.exp2(lc - top) for lc in ls]
            numer = sum(w * num_ref[c, sl, :] for c, w in enumerate(ws))
            o_ref[0, sl, :] = (numer / sum(ws)).astype(o_ref.dtype)


def _band_attention(qkv, *, batch, seq, q_col, k_col, v_col, kv_tile_of, configs, sink=None):
    n_tiles = 4
    qkv = qkv.reshape(batch, seq, qkv.shape[-1])
    tile = lambda col_of: pl.BlockSpec((1, seq, LANES), lambda b, t: (b, 0, col_of(t)))
    in_specs = [tile(lambda t: q_col // LANES + t),
                tile(lambda t: k_col // LANES + kv_tile_of(t)),
                tile(lambda t: v_col // LANES + kv_tile_of(t))]
    args = [qkv] * 3
    if sink is not None:
        sink_tiles = jnp.broadcast_to(sink.astype(F32).reshape(n_tiles, 2, 1), (n_tiles, 2, LANES))
        in_specs.append(pl.BlockSpec((1, 2, LANES), lambda b, t: (t, 0, 0)))
        args.append(sink_tiles)
    scratch = [pltpu.VMEM((2, seq, LANES), BF16), pltpu.VMEM((seq, LANES), BF16),
               pltpu.VMEM((seq, LANES), BF16)]
    if len(configs) > 1:
        scratch += [pltpu.VMEM((len(configs), seq, LANES), F32)] * 2
    out = pl.pallas_call(
        functools.partial(_band_kernel, configs=configs, seq=seq, has_sink=sink is not None),
        grid=(batch, n_tiles),
        in_specs=in_specs,
        out_specs=pl.BlockSpec((1, seq, LANES), lambda b, t: (b, 0, t)),
        out_shape=jax.ShapeDtypeStruct((batch, seq, n_tiles * LANES), BF16),
        scratch_shapes=scratch,
        compiler_params=_params("parallel", "arbitrary"),
        name="band_attention",
    )(*args)
    return out.reshape(batch * seq, n_tiles * LANES)


DSA_CHUNK = 2 * BLOCK
BOUND_SLACK = 1.03
DEN_FLOOR = 2.0 ** -60
NEG_BIG = -(2.0 ** 100)


def _fold_rows(x, rows):
    parts = [x[r:r + rows] for r in range(0, x.shape[0], rows)]
    while len(parts) > 1:
        parts = [a + b for a, b in zip(parts[::2], parts[1::2])] + (parts[-1:] if len(parts) % 2 else [])
    return parts[0]


def _dsa_pipe_kernel(qc0_ref, qc1_ref, qi0_ref, qi1_ref, wi0_ref, wi1_ref, ki_ref, kc_ref, vc_ref, o_ref,
                     qsc_ref, qsi_ref, wt_ref, key_ref, hi_ref, lo_ref, thr_ref, acc_ref, s_ref,
                     *, k_sel, n_tiles, n_steps):
    t = pl.program_id(0)
    n_heads = C_HEADS
    ch = DSA_CHUNK
    pairs = n_tiles // 2
    n_chunks = (n_tiles * BLOCK) // ch + 1
    max_n0 = (pairs * BLOCK + ch - 1) // ch
    wide = n_heads * LANES
    par1 = t % 2
    par2 = 1 - par1
    j1 = jnp.minimum(t, n_steps - 2) % pairs
    j2 = jnp.maximum(t - 1, 0) % pairs

    @pl.when(t == 0)
    def _():
        qsc_ref[1] = jnp.zeros(qsc_ref.shape[1:], BF16)
        eye = (lax.broadcasted_iota(I32, (LANES, LANES), 0)
               == lax.broadcasted_iota(I32, (LANES, LANES), 1)).astype(F32).astype(BF16)
        for par in range(2):
            for slot in range(2):
                for h in range(n_heads):
                    qsc_ref[par, slot, h * LANES:(h + 1) * LANES, LANES:2 * LANES] = eye
        key_ref[1] = jnp.zeros(key_ref.shape[1:], I32)
        thr_ref[1] = jnp.zeros(thr_ref.shape[1:], I32)

    def chunk_infos(j):
        n0 = (j * BLOCK + BLOCK + ch - 1) // ch
        tiles = (j, n_tiles - 1 - j)
        out = []
        for c in range(n_chunks):
            if c == 0:
                is1, slot, base = False, 0, 0
            elif c >= max_n0:
                is1, slot, base = True, 1, n0
            else:
                is1 = c >= n0
                slot, base = jnp.where(is1, 1, 0), jnp.where(is1, n0, 0)
            k0 = pl.multiple_of((c - base) * ch, ch)
            q0 = (tiles[slot] if isinstance(is1, bool) else jnp.where(is1, tiles[1], tiles[0])) * BLOCK
            out.append((is1, slot, k0, q0))
        return n0, tiles, out

    def pick(is1, a):
        return a[int(is1)] if isinstance(is1, bool) else jnp.where(is1, a[1], a[0])

    def add_to_slot(acc, is1, part):
        if isinstance(is1, bool):
            acc[int(is1)] = acc[int(is1)] + part
        else:
            zero = jnp.zeros_like(part)
            acc[0] = acc[0] + jnp.where(is1, zero, part)
            acc[1] = acc[1] + jnp.where(is1, part, zero)

    lane = lax.broadcasted_iota(I32, (1, LANES), 1)
    head_mask = [jnp.where(lane < HEAD_DIM, 1.0, 0.0).astype(BF16),
                 jnp.where(lane < HEAD_DIM, 0.0, 1.0).astype(BF16)]

    n0_2, tiles2, info2 = chunk_infos(j2)
    thr2 = [thr_ref[par2, s, 0:1, :] for s in range(2)]

    def logits(c):
        _, slot, k0, _ = info2[c]
        return lax.dot_general(kc_ref[pl.ds(k0, ch), :], qsc_ref[par2, slot, :, 0:LANES],
                               (((1,), (1,)), ((), ())), preferred_element_type=F32)

    def selected(c):
        return jnp.tile(key_ref[par2, c] >= pick(info2[c][0], thr2), (1, n_heads))

    def weighted_values(c, p):
        return lax.dot_general(vc_ref[pl.ds(info2[c][2], ch), :], p.astype(BF16),
                               (((0,), (0,)), ((), ())), preferred_element_type=F32)

    ksq = kc_ref[...].astype(F32)
    ksq = 0.5 * jnp.max(jnp.sum(ksq * ksq, axis=1, keepdims=True), axis=0, keepdims=True)
    ones8 = jnp.ones((8, LANES), BF16)
    bound = []
    for slot in range(2):
        q2 = qsc_ref[par2, slot, :, 0:LANES].astype(F32)
        qsq = lax.dot_general(ones8, (q2 * q2).astype(BF16), (((1,), (1,)), ((), ())),
                              preferred_element_type=F32)[0:1, :]
        per_head = jnp.sqrt(qsq * ksq) * BOUND_SLACK
        bound.append(functools.reduce(
            jnp.maximum, [per_head[:, h * LANES:(h + 1) * LANES] for h in range(n_heads)]))
    acc_ref[...] = jnp.zeros_like(acc_ref)

    def chunk_place(c):
        is1 = c >= n0_2
        return is1, jnp.where(is1, 1, 0), pl.multiple_of((c - jnp.where(is1, n0_2, 0)) * ch, ch)

    def logits_stage(c):
        is1, slot, k0 = chunk_place(c)
        sel = key_ref[par2, c] >= jnp.where(is1, thr2[1], thr2[0])
        bias = jnp.where(sel, -jnp.where(is1, bound[1], bound[0]), NEG_BIG).astype(BF16)
        lhs = jnp.concatenate([kc_ref[pl.ds(k0, ch), :], bias], axis=1)
        s_ref[c % 2] = lax.dot_general(lhs, qsc_ref[par2, slot], (((1,), (1,)), ((), ())),
                                       preferred_element_type=F32)

    def values_stage(c):
        _, slot, k0 = chunk_place(c)
        acc_ref[slot] += lax.dot_general(vc_ref[pl.ds(k0, ch), :], jnp.exp2(s_ref[c % 2]).astype(BF16),
                                         (((0,), (0,)), ((), ())), preferred_element_type=F32)

    def attention_chunk(c):
        values_stage(c)
        nxt = jnp.minimum(c + 1, n_chunks - 1)
        logits_stage(nxt)
        never = lambda x: jnp.where((pltpu.bitcast(x, I32) | 1) == 0, 1, 0)
        return never(acc_ref[0, 0:1, 0:LANES]), never(s_ref[nxt % 2, 0:1, 0:LANES])

    logits_stage(0)

    n0_1, tiles1, info1 = chunk_infos(j1)
    for slot, (qc_ref, qi_ref, wi_ref) in enumerate(((qc0_ref, qi0_ref, wi0_ref),
                                                     (qc1_ref, qi1_ref, wi1_ref))):
        for h in range(n_heads):
            tl = slice((h // 2) * LANES, (h // 2 + 1) * LANES)
            rows_h = slice(h * LANES, (h + 1) * LANES)
            qsc_ref[par1, slot, rows_h, 0:LANES] = qc_ref[:, tl] * head_mask[h % 2]
            qsi_ref[slot, rows_h, :] = qi_ref[:, tl] * head_mask[h % 2]
        wt_ref[slot] = wi_ref[...].T

    row_minus_col = (lax.broadcasted_iota(I32, (ch, LANES), 0)
                     - lax.broadcasted_iota(I32, (ch, LANES), 1))
    for c, (is1, slot, k0, q0) in enumerate(info1):
        d = lax.dot_general(ki_ref[pl.ds(k0, ch), :], qsi_ref[slot],
                            (((1,), (1,)), ((), ())), preferred_element_type=F32)
        sc = jnp.zeros((ch, LANES), F32)
        for h in range(IDX_HEADS):
            sc = sc + jnp.maximum(d[:, h * LANES:(h + 1) * LANES], 0.0) * wt_ref[slot, h:h + 1, :]
        bits = pltpu.bitcast(sc, I32)
        key = bits ^ ((bits >> 31) & jnp.int32(0x7FFFFFFF))
        key = jnp.where(sc == 0.0, 0, key)
        key = jnp.where(row_minus_col <= q0 - k0, key, INT_MIN)
        key_ref[par1, c] = key
        hi_ref[c] = (key >> 16).astype(I16)
        lo_ref[c] = ((key & 0xFFFF) - 32768).astype(I16)

    def count16(ref, cands, strict=False):
        acc = [jnp.zeros((16, LANES), I16), jnp.zeros((16, LANES), I16)]
        for c, (is1, _, _, _) in enumerate(info1):
            cand = pick(is1, cands)
            blk = ref[c]
            hit = jnp.where(blk > cand if strict else blk >= cand, jnp.int16(1), jnp.int16(0))
            add_to_slot(acc, is1, _fold_rows(hit, 16))
        return [jnp.sum(a.astype(I32), axis=0, keepdims=True) for a in acc]

    bits_per_trip = 4

    def search16(ref, need, first_chunk, trips):
        assert trips * bits_per_trip >= 16

        def trip(i, tu):
            zero_a, zero_b = attention_chunk(first_chunk + i)
            for b in range(bits_per_trip):
                it = i * bits_per_trip + b
                bit = jnp.where(it < 16, lax.shift_left(jnp.int32(1), jnp.maximum(15 - it, 0)), 0)
                cu = [u | bit for u in tu]
                cnt = count16(ref, [(u - 32768).astype(I16) for u in cu])
                tu = tuple(jnp.where(cnt[s] >= need[s], cu[s], tu[s]) for s in range(2))
            return tuple(u | zero_a | zero_b for u in tu)

        z = jnp.zeros((1, LANES), I32)
        tu = lax.fori_loop(0, trips, trip, (z, z))
        return [u - 32768 for u in tu]

    trips_hi = (n_chunks + 1) // 2
    ksel = jnp.full((1, LANES), k_sel, I32)
    p1 = search16(hi_ref, [ksel, ksel], 0, trips_hi)
    p1_16 = [p.astype(I16) for p in p1]
    above = count16(hi_ref, p1_16, strict=True)
    for c, (is1, _, _, _) in enumerate(info1):
        lo_ref[c] = jnp.where(hi_ref[c] == pick(is1, p1_16), lo_ref[c], jnp.int16(I16_MIN))
    p2 = search16(lo_ref, [ksel - above[0], ksel - above[1]], trips_hi, n_chunks - trips_hi)
    thr1 = [jnp.maximum((p1[s] << 16) | ((p2[s] + 32768) & 0xFFFF), INT_MIN + 1) for s in range(2)]
    for s in range(2):
        thr_ref[par1, s] = jnp.broadcast_to(thr1[s], (8, LANES))

    def count32(pred):
        acc = [jnp.zeros((8, LANES), I32), jnp.zeros((8, LANES), I32)]
        for c, (is1, _, _, _) in enumerate(info1):
            hit = jnp.where(pred(key_ref[par1, c], pick(is1, thr1)), 1, 0)
            add_to_slot(acc, is1, jnp.sum(hit.reshape(ch // 8, 8, LANES), axis=0))
        return [jnp.sum(a, axis=0, keepdims=True) for a in acc]

    c_ge = count32(lambda k, th: k >= th)
    over = jnp.max(jnp.maximum(c_ge[0], c_ge[1])) > k_sel
    den_min = jnp.min(jnp.minimum(acc_ref[0, HEAD_DIM:HEAD_DIM + 1, :],
                                  acc_ref[1, HEAD_DIM:HEAD_DIM + 1, :]))

    @pl.when(over)
    def _():
        c_gt = count32(lambda k, th: k > th)
        need = [(k_sel - c).astype(F32) for c in c_gt]
        tri = (lax.broadcasted_iota(I32, (ch, ch), 1)
               <= lax.broadcasted_iota(I32, (ch, ch), 0)).astype(BF16)
        carry = jnp.zeros((1, LANES), F32)
        for c, (is1, _, _, _) in enumerate(info1):
            carry = jnp.where(n0_1 == c, 0.0, carry)
            key = key_ref[par1, c]
            eq = jnp.where(key == pick(is1, thr1), 1.0, 0.0)
            pre = jnp.dot(tri, eq.astype(BF16), preferred_element_type=F32) + carry
            drop = (eq > 0.0) & (pre > pick(is1, need))
            key_ref[par1, c] = jnp.where(drop, INT_MIN, key)
            carry = carry + jnp.sum(eq, axis=0, keepdims=True)

    @pl.when(jnp.logical_not(den_min > DEN_FLOOR))
    def _():
        acc_ref[...] = jnp.zeros_like(acc_ref)
        m = [jnp.full((1, wide), -jnp.inf, F32)] * 2
        for c, (is1, slot, _, _) in enumerate(info2):
            s = jnp.where(selected(c), logits(c), -jnp.inf)
            m_old = pick(is1, m)
            m_new = jnp.maximum(m_old, jnp.max(s, axis=0, keepdims=True))
            m_safe = jnp.where(m_new == -jnp.inf, 0.0, m_new)
            acc_ref[slot] = acc_ref[slot] * jnp.exp2(m_old - m_safe) + weighted_values(c, jnp.exp2(s - m_safe))
            if isinstance(is1, bool):
                m[int(is1)] = m_new
            else:
                m = [jnp.where(is1, m[0], m_new), jnp.where(is1, m_new, m[1])]

    lo2 = lax.broadcasted_iota(I32, (BLOCK, LANES), 1) < HEAD_DIM
    for slot in range(2):
        acc = acc_ref[slot]
        o_t = acc / acc[HEAD_DIM:HEAD_DIM + 1, :]
        for tp in range(n_heads // 2):
            ev = o_t[:, (2 * tp) * LANES:(2 * tp + 1) * LANES].T
            od = o_t[:, (2 * tp + 1) * LANES:(2 * tp + 2) * LANES].T
            o_ref[0, tiles2[slot], :, tp * LANES:(tp + 1) * LANES] = (
                jnp.where(lo2, ev, pltpu.roll(od, HEAD_DIM, 1)).astype(o_ref.dtype))


def _dsa_pipe(c_pack, wi, batch, seq):
    n = batch * seq
    nq = seq // BLOCK
    assert nq % 2 == 0 and (nq * BLOCK) % DSA_CHUNK == 0
    pairs = nq // 2
    n_steps = batch * pairs + 1
    n_chunks = (nq * BLOCK) // DSA_CHUNK + 1
    k_sel = min(TOPK_MAX, seq // 4)
    qw = C_HEADS * HEAD_DIM
    sel_step = lambda t: jnp.minimum(t, n_steps - 2)
    first = lambda t: (sel_step(t) // pairs) * nq + sel_step(t) % pairs
    second = lambda t: (sel_step(t) // pairs) * nq + nq - 1 - sel_step(t) % pairs
    sel_row = lambda t: sel_step(t) // pairs
    att_row = lambda t: jnp.maximum(t - 1, 0) // pairs
    qrow = lambda blk, which: pl.BlockSpec((BLOCK, qw), lambda t: (which(t), blk))
    wrow = lambda which: pl.BlockSpec((BLOCK, LANES), lambda t: (which(t), 0))
    full = lambda col, which: pl.BlockSpec((seq, LANES), lambda t: (which(t), col // LANES))
    out = pl.pallas_call(
        functools.partial(_dsa_pipe_kernel, k_sel=k_sel, n_tiles=nq, n_steps=n_steps),
        grid=(n_steps,),
        in_specs=[qrow(0, first), qrow(0, second), qrow(1, first), qrow(1, second),
                  wrow(first), wrow(second),
                  full(C_KI, sel_row), full(C_KC, att_row), full(C_VC, att_row)],
        out_specs=pl.BlockSpec((1, nq, BLOCK, qw), lambda t: (att_row(t), 0, 0, 0)),
        out_shape=jax.ShapeDtypeStruct((batch, nq, BLOCK, qw), BF16),
        scratch_shapes=[
            pltpu.VMEM((2, 2, C_HEADS * LANES, 2 * LANES), BF16),
            pltpu.VMEM((2, IDX_HEADS * LANES, LANES), BF16),
            pltpu.VMEM((2, LANES, LANES), F32),
            pltpu.VMEM((2, n_chunks, DSA_CHUNK, LANES), I32),
            pltpu.VMEM((n_chunks, DSA_CHUNK, LANES), I16),
            pltpu.VMEM((n_chunks, DSA_CHUNK, LANES), I16),
            pltpu.VMEM((2, 2, 8, LANES), I32),
            pltpu.VMEM((2, LANES, C_HEADS * LANES), F32),
            pltpu.VMEM((2, DSA_CHUNK, C_HEADS * LANES), F32),
        ],
        compiler_params=_params("arbitrary"),
        name="dsa_pipe",
    )(c_pack, c_pack, c_pack, c_pack, wi, wi, c_pack, c_pack, c_pack)
    return out.reshape(n, qw)


def _merge_kernel(x_ref, oa_ref, ob_ref, oc_ref,
                  wg_ref, wa_ref, wb_ref, wc_ref, wo_ref, g_ref, b_ref, o_ref, *, alpha):
    d = x_ref.shape[1]
    for r0 in range(0, x_ref.shape[0], FFN_SUB):
        rows = slice(r0, r0 + FFN_SUB)
        x = x_ref[rows, :]
        xb = x.astype(BF16)

        def branch(o_ref_, w_ref, k):
            gate = jnp.dot(xb, wg_ref[:, k * d:(k + 1) * d], preferred_element_type=F32)
            return jax.nn.sigmoid(gate) * jnp.dot(o_ref_[rows, :], w_ref[...], preferred_element_type=F32)

        merged = branch(oa_ref, wa_ref, 0) + branch(ob_ref, wb_ref, 1) + branch(oc_ref, wc_ref, 2)
        out = jnp.dot(merged.astype(BF16), wo_ref[...], preferred_element_type=F32)
        o_ref[rows, :] = _layer_norm(alpha * x + out, g_ref[...], b_ref[...])


def _merge(x, oa, ob, oc, w_gates, w_a, w_b, w_c, w_o, g, b, alpha):
    n, d = x.shape
    tm = 1024
    bw = oa.shape[1]
    row = lambda w: pl.BlockSpec((tm, w), lambda i: (i, 0))
    const = lambda a: _resident(a.shape, lambda i: (0, 0))
    return pl.pallas_call(
        functools.partial(_merge_kernel, alpha=alpha),
        grid=(n // tm,),
        in_specs=[row(d)] + [row(bw)] * 3
        + [const(w_gates), const(w_a), const(w_b), const(w_c), const(w_o), const(g), const(b)],
        out_specs=row(d),
        out_shape=jax.ShapeDtypeStruct((n, d), F32),
        compiler_params=_params("parallel"),
        name="merge_ln",
    )(x, oa, ob, oc, w_gates, w_a, w_b, w_c, w_o, g, b)


def _mixer(x, cos, sin, w_in, sink_b, w_br_a, w_br_b, w_br_c, w_out, g, b, alpha, batch, seq):
    n = batch * seq
    w_packed, w_gates = _pack_in_weights(w_in)
    a_pack, b_pack, c_pack, wi = _inproj(x, w_packed, cos, sin)

    oa = _band_attention(a_pack, batch=batch, seq=seq, q_col=0, k_col=A_W, v_col=2 * A_W,
                         kv_tile_of=lambda t: t,
                         configs=tuple((dil, window // dil) for window, dil in A_CONFIGS))
    ob = _band_attention(b_pack, batch=batch, seq=seq, q_col=0, k_col=512, v_col=768,
                         kv_tile_of=lambda t: t // 2, configs=((1, B_WINDOW - 1),), sink=sink_b)
    oc = _dsa_pipe(c_pack, wi, batch, seq)

    return _merge(x, oa, ob, oc, w_gates, w_br_a.astype(BF16), w_br_b.astype(BF16),
                  w_br_c.astype(BF16), w_out.astype(BF16), g, b, alpha)


def kernel(x, positions, w_in, sink_b, w_br_a, w_br_b, w_br_c, w_out,
           ffn1_in, ffn1_out, ffn2_in, ffn2_out, ln_g, ln_b):
    batch, seq, d = x.shape
    depth = w_in.shape[0]
    alpha = float((2 * depth) ** 0.25)
    cos, sin = _rope_tables(positions)
    h = x.reshape(batch * seq, d)
    for l in range(depth):
        gb = lambda k: (ln_g[l, k][None, :], ln_b[l, k][None, :])
        h = _ffn(h, ffn1_in[l].astype(BF16), ffn1_out[l].astype(BF16), *gb(0), alpha)
        h = _mixer(h, cos, sin, w_in[l], sink_b[l], w_br_a[l], w_br_b[l], w_br_c[l], w_out[l],
                   *gb(1), alpha, batch, seq)
        h = _ffn(h, ffn2_in[l].astype(BF16), ffn2_out[l].astype(BF16), *gb(2), alpha)
    return h.reshape(batch, seq, d)
```
